```python
import jax, jax.numpy as jnp
from jax import lax
import numpy as np

D_MODEL = 1024
BATCH = 8
SEQ = 2048
DEPTH = 4

N_MIXERS = 3
D_FF = 2816
RMS_EPS = 1e-6
LN_EPS = 1e-5
D_RNN = 1280
LRU_HEADS = 10
LRU_BLOCK = D_RNN // LRU_HEADS
LRU_CONV = 4
LRU_C = 8.0
SB_HEADS = 16
SB_HEAD_DIM = D_MODEL // SB_HEADS
SB_BLOCK = 128
CONV_WIDTH = 31

N_A = (DEPTH + 2) // 3
N_B = (DEPTH + 1) // 3
N_C = DEPTH // 3

kernel_name = "hybrid_rglru_stickbreak_conformer_macaron"


def rmsnorm(x, g):
    xf = x.astype(jnp.float32)
    y = xf * lax.rsqrt(jnp.mean(xf * xf, axis=-1, keepdims=True) + RMS_EPS)
    return (y * g.astype(jnp.float32)).astype(x.dtype)


def layernorm(x, g, b):
    xf = x.astype(jnp.float32)
    mu = jnp.mean(xf, axis=-1, keepdims=True)
    xc = xf - mu
    y = xc * lax.rsqrt(jnp.mean(xc * xc, axis=-1, keepdims=True) + LN_EPS)
    return (y * g.astype(jnp.float32) + b.astype(jnp.float32)).astype(x.dtype)


def causal_dwconv(x, w, b):
    k_width, ch = w.shape
    out = lax.conv_general_dilated(
        x, w[:, None, :].astype(x.dtype), window_strides=(1,),
        padding=((k_width - 1, 0),), dimension_numbers=("NWC", "WIO", "NWC"),
        feature_group_count=ch)
    return out + b.astype(x.dtype)


def swiglu(x, w_in, w_out):
    g, u = jnp.split(x @ w_in, 2, axis=-1)
    return (jax.nn.silu(g) * u) @ w_out


def rglru_block(x, w_in, conv_w, conv_b, gate_w, gate_b, lam, w_out):
    bsz, t_len, _ = x.shape
    gate, xr = jnp.split(x @ w_in, 2, axis=-1)
    xr = causal_dwconv(xr, conv_w, conv_b)
    xh = xr.reshape(bsz, t_len, LRU_HEADS, LRU_BLOCK)
    gates = jnp.einsum("bthi,ghij->gbthj", xh, gate_w) + gate_b[:, None, None]
    gates = jax.nn.sigmoid(gates.astype(jnp.float32)).reshape(2, bsz, t_len, D_RNN)
    r_t, i_t = gates[0], gates[1]
    log_a = -LRU_C * r_t * jax.nn.softplus(-lam.astype(jnp.float32))
    a_t = jnp.exp(log_a)
    b_t = jnp.sqrt(-jnp.expm1(2.0 * log_a)) * (i_t * xr.astype(jnp.float32))

    def combine(left, right):
        a1, b1 = left
        a2, b2 = right
        return a1 * a2, a2 * b1 + b2

    _, h = lax.associative_scan(combine, (a_t, b_t), axis=1)
    y = h.astype(x.dtype) * jax.nn.gelu(gate)
    return y @ w_out


def stick_breaking_attention(x, w_qkv, w_o):
    bsz, t_len, _ = x.shape
    q, k, v = jnp.split(x @ w_qkv, 3, axis=-1)

    def heads(z):
        return z.reshape(bsz, t_len, SB_HEADS, SB_HEAD_DIM).transpose(0, 2, 1, 3).astype(jnp.float32)

    q = heads(q) * (SB_HEAD_DIM ** -0.5)
    k = heads(k)
    v = heads(v)
    n_blocks = t_len // SB_BLOCK
    q_blocks = q.reshape(bsz, SB_HEADS, n_blocks, SB_BLOCK, SB_HEAD_DIM).transpose(2, 0, 1, 3, 4)
    starts = jnp.arange(n_blocks, dtype=jnp.int32) * SB_BLOCK
    key_pos = jnp.arange(t_len, dtype=jnp.int32)

    def block(args):
        q_blk, start = args
        z = jnp.einsum("bhqd,bhkd->bhqk", q_blk, k)
        q_pos = start + jnp.arange(SB_BLOCK, dtype=jnp.int32)
        mask = key_pos[None, :] < q_pos[:, None]
        log_keep = jnp.where(mask, -jax.nn.softplus(z), 0.0)
        later = lax.cumsum(log_keep, axis=3, reverse=True) - log_keep
        w = jnp.where(mask, jnp.exp(jax.nn.log_sigmoid(z) + later), 0.0)
        return jnp.einsum("bhqk,bhkd->bhqd", w, v)

    o = lax.map(block, (q_blocks, starts))
    o = o.transpose(1, 0, 3, 2, 4).reshape(bsz, t_len, D_MODEL)
    return o.astype(x.dtype) @ w_o


def conformer_conv(x, w_in, b_in, conv_w, conv_b, ln_g, ln_b, w_out, b_out):
    val, gt = jnp.split(x @ w_in + b_in, 2, axis=-1)
    h = val * jax.nn.sigmoid(gt)
    h = causal_dwconv(h, conv_w, conv_b)
    h = jax.nn.silu(layernorm(h, ln_g, ln_b))
    return h @ w_out + b_out


def setup_inputs(seed: int = 0) -> dict:
    key = jax.random.key(seed)
    ks = jax.random.split(key, 24)

    def dense(k, shape, fan_in):
        return jax.random.normal(k, shape, jnp.float32) * (fan_in ** -0.5)

    def small(k, shape):
        return jax.random.normal(k, shape, jnp.float32) * 0.02

    u = jax.random.uniform(ks[9], (N_A, D_RNN), jnp.float32, minval=0.9, maxval=0.999)
    return {
        "x": jax.random.normal(ks[0], (BATCH, SEQ, D_MODEL), jnp.float32),
        "norm_g": 1.0 + small(ks[1], (DEPTH, 3, D_MODEL)),
        "ffn_w_in": dense(ks[2], (DEPTH, 2, D_MODEL, 2 * D_FF), D_MODEL),
        "ffn_w_out": dense(ks[3], (DEPTH, 2, D_FF, D_MODEL), D_FF),
        "a_w_in": dense(ks[4], (N_A, D_MODEL, 2 * D_RNN), D_MODEL),
        "a_conv_w": dense(ks[5], (N_A, LRU_CONV, D_RNN), LRU_CONV),
        "a_conv_b": small(ks[6], (N_A, D_RNN)),
        "a_gate_w": dense(ks[7], (N_A, 2, LRU_HEADS, LRU_BLOCK, LRU_BLOCK), LRU_BLOCK),
        "a_gate_b": small(ks[8], (N_A, 2, LRU_HEADS, LRU_BLOCK)),
        "a_lambda": jnp.log(u) - jnp.log1p(-u),
        "a_w_out": dense(ks[10], (N_A, D_RNN, D_MODEL), D_RNN),
        "b_w_qkv": dense(ks[11], (N_B, D_MODEL, 3 * D_MODEL), D_MODEL),
        "b_w_o": dense(ks[12], (N_B, D_MODEL, D_MODEL), D_MODEL),
        "c_w_in": dense(ks[13], (N_C, D_MODEL, 2 * D_MODEL), D_MODEL),
        "c_b_in": small(ks[14], (N_C, 2 * D_MODEL)),
        "c_conv_w": dense(ks[15], (N_C, CONV_WIDTH, D_MODEL), CONV_WIDTH),
        "c_conv_b": small(ks[16], (N_C, D_MODEL)),
        "c_ln_g": 1.0 + small(ks[17], (N_C, D_MODEL)),
        "c_ln_b": small(ks[18], (N_C, D_MODEL)),
        "c_w_out": dense(ks[19], (N_C, D_MODEL, D_MODEL), D_MODEL),
        "c_b_out": small(ks[20], (N_C, D_MODEL)),
        "final_g": 1.0 + small(ks[21], (D_MODEL,)),
    }


def reference(x, norm_g, ffn_w_in, ffn_w_out, a_w_in, a_conv_w, a_conv_b, a_gate_w, a_gate_b,
              a_lambda, a_w_out, b_w_qkv, b_w_o, c_w_in, c_b_in, c_conv_w, c_conv_b, c_ln_g,
              c_ln_b, c_w_out, c_b_out, final_g):
    for layer in range(DEPTH):
        kind, j = layer % N_MIXERS, layer // N_MIXERS
        x = x + 0.5 * swiglu(rmsnorm(x, norm_g[layer, 0]), ffn_w_in[layer, 0], ffn_w_out[layer, 0])
        hn = rmsnorm(x, norm_g[layer, 1])
        if kind == 0:
            m = rglru_block(hn, a_w_in[j], a_conv_w[j], a_conv_b[j], a_gate_w[j], a_gate_b[j],
                            a_lambda[j], a_w_out[j])
        elif kind == 1:
            m = stick_breaking_attention(hn, b_w_qkv[j], b_w_o[j])
        else:
            m = conformer_conv(hn, c_w_in[j], c_b_in[j], c_conv_w[j], c_conv_b[j], c_ln_g[j],
                               c_ln_b[j], c_w_out[j], c_b_out[j])
        x = x + m
        x = x + 0.5 * swiglu(rmsnorm(x, norm_g[layer, 2]), ffn_w_in[layer, 1], ffn_w_out[layer, 1])
    return rmsnorm(x, final_g)
```

```python
import functools

import jax
import jax.numpy as jnp
from jax import lax
from jax.experimental import pallas as pl
from jax.experimental.pallas import tpu as pltpu

F32 = jnp.float32
BF16 = jnp.bfloat16

RMS_EPS = 1e-6
LN_EPS = 1e-5
LRU_C = 8.0
SB_HEAD_DIM = 64

V7X_LANES = 128
V7X_SUBLANES = 8
V7X_MXU_DIM = 256
V7X_VMEM_LIMIT_BYTES = 56 * 1024 * 1024

FFN_ROWS = 1024
FFN_CHUNK = 512
SEQ_ROWS = 512
CONV_ROWS = 64
ATT_BLOCK = 128


def _cparams(*sem):
    return pltpu.CompilerParams(dimension_semantics=sem, vmem_limit_bytes=V7X_VMEM_LIMIT_BYTES)


def _resident(shape):
    zeros = (0,) * len(shape)
    return pl.BlockSpec(shape, lambda *_: zeros, pipeline_mode=pl.Buffered(1))


def _rms(x, g):
    ms = jnp.mean(x * x, axis=-1, keepdims=True)
    return x * lax.rsqrt(ms + RMS_EPS) * g


def _dot(a, b):
    return jnp.dot(a, b, preferred_element_type=F32)


def _softplus(z):
    return jnp.maximum(z, 0.0) + jnp.log(1.0 + jnp.exp(-jnp.abs(z)))


def _neg_expm1(y):
    t = jnp.tanh(0.5 * y)
    small = -2.0 * t / (1.0 - t)
    return jnp.where(y > -0.5, small, 1.0 - jnp.exp(y))


def _ffn_kernel(x_ref, g_ref, win_ref, wout_ref, *rest, d_ff, final):
    if final:
        fg_ref, o_ref, act_ref = rest
    else:
        o_ref, act_ref = rest
    x = x_ref[...]
    xn = _rms(x, g_ref[...]).astype(BF16)
    for c0 in range(0, d_ff, FFN_CHUNK):
        cw = min(FFN_CHUNK, d_ff - c0)
        gate = _dot(xn, win_ref[:, c0:c0 + cw])
        up = _dot(xn, win_ref[:, d_ff + c0:d_ff + c0 + cw])
        act_ref[:, c0:c0 + cw] = (gate * jax.nn.sigmoid(gate) * up).astype(BF16)
    out = x + 0.5 * _dot(act_ref[...], wout_ref[...])
    if final:
        out = _rms(out, fg_ref[...])
    o_ref[...] = out


def _ffn(x2, g, w_in, w_out, final_g=None):
    n, d = x2.shape
    d_ff = w_out.shape[0]
    tm = min(FFN_ROWS, n)
    final = final_g is not None
    row_spec = pl.BlockSpec((tm, d), lambda i: (i, 0))
    in_specs = [row_spec, _resident((1, d)), _resident(w_in.shape), _resident(w_out.shape)]
    args = [x2, g.reshape(1, d), w_in, w_out]
    if final:
        in_specs.append(_resident((1, d)))
        args.append(final_g.reshape(1, d))
    return pl.pallas_call(
        functools.partial(_ffn_kernel, d_ff=d_ff, final=final),
        grid=(n // tm,),
        in_specs=in_specs,
        out_specs=row_spec,
        out_shape=jax.ShapeDtypeStruct((n, d), F32),
        scratch_shapes=[pltpu.VMEM((tm, d_ff), BF16)],
        compiler_params=_cparams("arbitrary"),
        name="ffn_final" if final else "ffn",
    )(*args)


def _rglru_kernel(x_ref, g_ref, win_ref, cw_ref, cb_ref, gw_ref, gb_ref, lam_ref, wout_ref, o_ref,
                  xbuf, a_s, b_s, hcar, *, tt, d_rnn, conv_k):
    pair = 2 * V7X_LANES
    sub = V7X_SUBLANES

    @pl.when(pl.program_id(1) == 0)
    def _():
        xbuf[0:sub, :] = jnp.zeros((sub, d_rnn), F32)
        hcar[...] = jnp.zeros_like(hcar)

    x = x_ref[...]
    hn = _rms(x, g_ref[...]).astype(BF16)
    gate = _dot(hn, win_ref[:, :d_rnn])
    xr = _dot(hn, win_ref[:, d_rnn:])

    xbuf[sub:sub + tt, :] = xr
    xc = cb_ref[...] + cw_ref[conv_k - 1:conv_k, :] * xr
    for s in range(1, conv_k):
        xc = xc + cw_ref[conv_k - 1 - s:conv_k - s, :] * xbuf[sub - s:sub - s + tt, :]
    xbuf[0:sub, :] = xbuf[tt:tt + sub, :]

    xcb = xc.astype(BF16)
    sp_lam = _softplus(-lam_ref[...])
    for p in range(d_rnn // pair):
        cols = slice(p * pair, (p + 1) * pair)
        gp = _dot(xcb[:, cols], gw_ref[p])
        r_t = jax.nn.sigmoid(gp[:, :pair] + gb_ref[0:1, cols])
        i_t = jax.nn.sigmoid(gp[:, pair:] + gb_ref[1:2, cols])
        log_a = (-LRU_C) * r_t * sp_lam[:, cols]
        a_s[:, cols] = jnp.exp(log_a)
        b_s[:, cols] = jnp.sqrt(_neg_expm1(2.0 * log_a)) * (i_t * xc[:, cols])

    row = lax.broadcasted_iota(jnp.int32, (sub, d_rnn), 0)

    def group(j, hprev):
        r0 = pl.multiple_of(j * sub, sub)
        a = a_s[pl.ds(r0, sub), :]
        b = b_s[pl.ds(r0, sub), :]
        s = 1
        while s < sub:
            keep = row >= s
            a_sh = jnp.where(keep, pltpu.roll(a, s, 0), 1.0)
            b_sh = jnp.where(keep, pltpu.roll(b, s, 0), 0.0)
            b = a * b_sh + b
            a = a * a_sh
            s *= 2
        h = a * hprev + b
        b_s[pl.ds(r0, sub), :] = h
        return h[sub - 1:sub, :]

    hcar[...] = lax.fori_loop(0, tt // sub, group, hcar[...])

    y = (b_s[...] * jax.nn.gelu(gate)).astype(BF16)
    o_ref[...] = x + _dot(y, wout_ref[...])


def _rglru(x, g, w_in, conv_w, conv_b, gate_w_packed, gate_b, lam, w_out):
    bsz, t_len, d = x.shape
    d_rnn = w_out.shape[0]
    conv_k = conv_w.shape[0]
    tt = min(SEQ_ROWS, t_len)
    x_spec = pl.BlockSpec((None, tt, d), lambda b, t: (b, t, 0))
    return pl.pallas_call(
        functools.partial(_rglru_kernel, tt=tt, d_rnn=d_rnn, conv_k=conv_k),
        grid=(bsz, t_len // tt),
        in_specs=[x_spec, _resident((1, d)), _resident(w_in.shape), _resident(conv_w.shape),
                  _resident((1, d_rnn)), _resident(gate_w_packed.shape), _resident((2, d_rnn)),
                  _resident((1, d_rnn)), _resident(w_out.shape)],
        out_specs=x_spec,
        out_shape=jax.ShapeDtypeStruct(x.shape, F32),
        scratch_shapes=[pltpu.VMEM((tt + V7X_SUBLANES, d_rnn), F32), pltpu.VMEM((tt, d_rnn), F32),
                        pltpu.VMEM((tt, d_rnn), F32), pltpu.VMEM((1, d_rnn), F32)],
        compiler_params=_cparams("arbitrary", "arbitrary"),
        name="rglru",
    )(x, g.reshape(1, d), w_in, conv_w, conv_b.reshape(1, d_rnn), gate_w_packed,
      gate_b.reshape(2, d_rnn), lam.reshape(1, d_rnn), w_out)


def _pack_gate_w(gate_w):
    _, heads, blk, _ = gate_w.shape
    z = jnp.zeros((blk, blk), gate_w.dtype)
    out = []
    for p in range(heads // 2):
        h0, h1 = 2 * p, 2 * p + 1
        top = jnp.concatenate([gate_w[0, h0], z, gate_w[1, h0], z], axis=1)
        bot = jnp.concatenate([z, gate_w[0, h1], z, gate_w[1, h1]], axis=1)
        out.append(jnp.concatenate([top, bot], axis=0))
    return jnp.stack(out).astype(BF16)


def _qkv_kernel(x_ref, g_ref, w_ref, q_ref, k_ref, v_ref, *, d, scale):
    hn = _rms(x_ref[...], g_ref[...]).astype(BF16)
    q_ref[...] = (_dot(hn, w_ref[:, :d]) * scale).astype(BF16)
    k_ref[...] = _dot(hn, w_ref[:, d:2 * d]).astype(BF16)
    v_ref[...] = _dot(hn, w_ref[:, 2 * d:]).astype(BF16)


def _qkv(x2, g, w_qkv):
    n, d = x2.shape
    tm = min(FFN_ROWS, n)
    row_spec = pl.BlockSpec((tm, d), lambda i: (i, 0))
    out = jax.ShapeDtypeStruct((n, d), BF16)
    return pl.pallas_call(
        functools.partial(_qkv_kernel, d=d, scale=SB_HEAD_DIM ** -0.5),
        grid=(n // tm,),
        in_specs=[row_spec, _resident((1, d)), _resident(w_qkv.shape)],
        out_specs=[row_spec, row_spec, row_spec],
        out_shape=[out, out, out],
        compiler_params=_cparams("arbitrary"),
        name="qkv",
    )(x2, g.reshape(1, d), w_qkv)


def _attn_kernel(q_ref, k_ref, v_ref, x_ref, wo_ref, o_ref, oacc_ref, *, blk, d):
    qi = pl.program_id(1)
    lanes = V7X_LANES
    head0 = lax.broadcasted_iota(jnp.int32, (1, lanes), 1) < SB_HEAD_DIM
    rows = lax.broadcasted_iota(jnp.int32, (2 * blk, blk), 0)
    rows = jnp.where(rows >= blk, rows - blk, rows)
    cols = lax.broadcasted_iota(jnp.int32, (2 * blk, blk), 1)
    diag_mask = cols < rows
    jj = lax.broadcasted_iota(jnp.int32, (blk, blk), 0)
    kk = lax.broadcasted_iota(jnp.int32, (blk, blk), 1)
    tri = jnp.concatenate([(jj >= kk).astype(BF16), jnp.ones((blk, lanes), BF16)], axis=1)

    for p in range(d // lanes):
        cs = slice(p * lanes, (p + 1) * lanes)
        qp = q_ref[:, cs]
        zero = jnp.zeros_like(qp)
        qs = jnp.concatenate([jnp.where(head0, qp, zero), jnp.where(head0, zero, qp)], axis=0)

        def block(kb, carry, masked):
            later, acc = carry
            k0 = pl.multiple_of(kb * blk, blk)
            kp = k_ref[pl.ds(k0, blk), cs]
            vp = v_ref[pl.ds(k0, blk), cs]
            z = lax.dot_general(qs, kp, (((1,), (1,)), ((), ())), preferred_element_type=F32)
            sp = _softplus(z)
            if masked:
                sp = jnp.where(diag_mask, sp, 0.0)
            hi = sp.astype(BF16)
            lo = (sp - hi.astype(F32)).astype(BF16)
            cum = _dot(jnp.concatenate([hi, lo], axis=0), tri)
            cum = cum[:2 * blk] + cum[2 * blk:]
            w = jnp.exp(z - (cum[:, :blk] + later))
            if masked:
                w = jnp.where(diag_mask, w, 0.0)
            return later + cum[:, blk:], acc + _dot(w.astype(BF16), vp)

        init = (jnp.zeros((2 * blk, lanes), F32), jnp.zeros((2 * blk, lanes), F32))
        carry = block(qi, init, True)
        _, acc = lax.fori_loop(0, qi, lambda j, c: block(qi - 1 - j, c, False), carry)
        oacc_ref[:, cs] = jnp.where(head0, acc[:blk], acc[blk:]).astype(BF16)

    o_ref[...] = x_ref[...] + _dot(oacc_ref[...], wo_ref[...])


def _attn(x, q, k, v, w_o):
    bsz, t_len, d = x.shape
    blk = ATT_BLOCK
    q_spec = pl.BlockSpec((None, blk, d), lambda b, i: (b, i, 0))
    kv_spec = pl.BlockSpec((None, t_len, d), lambda b, i: (b, 0, 0))
    return pl.pallas_call(
        functools.partial(_attn_kernel, blk=blk, d=d),
        grid=(bsz, t_len // blk),
        in_specs=[q_spec, kv_spec, kv_spec, q_spec, _resident(w_o.shape)],
        out_specs=q_spec,
        out_shape=jax.ShapeDtypeStruct(x.shape, F32),
        scratch_shapes=[pltpu.VMEM((blk, d), BF16)],
        compiler_params=_cparams("arbitrary", "arbitrary"),
        name="sb_attn",
    )(q, k, v, x, w_o)


def _conformer_kernel(x_ref, g_ref, win_ref, bin_ref, cw_ref, cb_ref, lng_ref, lnb_ref, wout_ref, bout_ref,
                      o_ref, shifted, conv_out, *, tt, d, conv_k, halo):
    sub = V7X_SUBLANES
    lanes = V7X_LANES

    @pl.when(pl.program_id(1) == 0)
    def _():
        shifted[0, 0:halo, :] = jnp.zeros((halo, d), F32)

    x = x_ref[...]
    hn = _rms(x, g_ref[...]).astype(BF16)
    val = _dot(hn, win_ref[:, :d]) + bin_ref[:, :d]
    gt = _dot(hn, win_ref[:, d:]) + bin_ref[:, d:]
    shifted[0, halo:halo + tt, :] = val * jax.nn.sigmoid(gt)
    for r in range(1, sub):
        shifted[r, sub:halo + tt, :] = shifted[0, sub - r:halo + tt - r, :]

    for lc in range(d // lanes):
        cs = slice(lc * lanes, (lc + 1) * lanes)

        def chunk(c, _, cs=cs):
            r0 = pl.multiple_of(c * CONV_ROWS, CONV_ROWS)
            acc = jnp.broadcast_to(cb_ref[:, cs], (CONV_ROWS, lanes))
            for s in range(conv_k):
                a, r = divmod(s, sub)
                tap = cw_ref[conv_k - 1 - s:conv_k - s, cs]
                acc = acc + tap * shifted[r, pl.ds(r0 + (halo - sub * a), CONV_ROWS), cs]
            conv_out[pl.ds(r0, CONV_ROWS), cs] = acc
            return 0

        lax.fori_loop(0, tt // CONV_ROWS, chunk, 0)

    shifted[0, 0:halo, :] = shifted[0, tt:tt + halo, :]

    c = conv_out[...]
    mu = jnp.mean(c, axis=-1, keepdims=True)
    cc = c - mu
    var = jnp.mean(cc * cc, axis=-1, keepdims=True)
    y = cc * lax.rsqrt(var + LN_EPS) * lng_ref[...] + lnb_ref[...]
    y = (y * jax.nn.sigmoid(y)).astype(BF16)
    o_ref[...] = x + _dot(y, wout_ref[...]) + bout_ref[...]


def _conformer(x, g, w_in, b_in, conv_w, conv_b, ln_g, ln_b, w_out, b_out):
    bsz, t_len, d = x.shape
    conv_k = conv_w.shape[0]
    halo = -(-(conv_k - 1) // V7X_SUBLANES) * V7X_SUBLANES
    tt = min(SEQ_ROWS // 2, t_len)
    x_spec = pl.BlockSpec((None, tt, d), lambda b, t: (b, t, 0))
    return pl.pallas_call(
        functools.partial(_conformer_kernel, tt=tt, d=d, conv_k=conv_k, halo=halo),
        grid=(bsz, t_len // tt),
        in_specs=[x_spec, _resident((1, d)), _resident(w_in.shape), _resident((1, 2 * d)),
                  _resident(conv_w.shape), _resident((1, d)), _resident((1, d)), _resident((1, d)),
                  _resident(w_out.shape), _resident((1, d))],
        out_specs=x_spec,
        out_shape=jax.ShapeDtypeStruct(x.shape, F32),
        scratch_shapes=[pltpu.VMEM((V7X_SUBLANES, tt + halo, d), F32), pltpu.VMEM((tt, d), F32)],
        compiler_params=_cparams("arbitrary", "arbitrary"),
        name="conformer",
    )(x, g.reshape(1, d), w_in, b_in.reshape(1, 2 * d), conv_w, conv_b.reshape(1, d), ln_g.reshape(1, d),
      ln_b.reshape(1, d), w_out, b_out.reshape(1, d))


def kernel(x, norm_g, ffn_w_in, ffn_w_out, a_w_in, a_conv_w, a_conv_b, a_gate_w, a_gate_b, a_lambda, a_w_out,
           b_w_qkv, b_w_o, c_w_in, c_b_in, c_conv_w, c_conv_b, c_ln_g, c_ln_b, c_w_out, c_b_out, final_g):
    bsz, t_len, d = x.shape
    depth = norm_g.shape[0]
    n_mixers = 3
    ffn_w_in = ffn_w_in.astype(BF16)
    ffn_w_out = ffn_w_out.astype(BF16)

    def flat(z):
        return z.reshape(bsz * t_len, d)

    for layer in range(depth):
        kind, j = layer % n_mixers, layer // n_mixers
        x = _ffn(flat(x), norm_g[layer, 0], ffn_w_in[layer, 0], ffn_w_out[layer, 0]).reshape(bsz, t_len, d)
        g = norm_g[layer, 1]
        if kind == 0:
            x = _rglru(x, g, a_w_in[j].astype(BF16), a_conv_w[j], a_conv_b[j], _pack_gate_w(a_gate_w[j]),
                       a_gate_b[j], a_lambda[j], a_w_out[j].astype(BF16))
        elif kind == 1:
            q, k, v = _qkv(flat(x), g, b_w_qkv[j].astype(BF16))
            shape = (bsz, t_len, d)
            x = _attn(x, q.reshape(shape), k.reshape(shape), v.reshape(shape), b_w_o[j].astype(BF16))
        else:
            x = _conformer(x, g, c_w_in[j].astype(BF16), c_b_in[j], c_conv_w[j], c_conv_b[j], c_ln_g[j],
                           c_ln_b[j], c_w_out[j].astype(BF16), c_b_out[j])
        last = layer == depth - 1
        x = _ffn(flat(x), norm_g[layer, 2], ffn_w_in[layer, 1], ffn_w_out[layer, 1],
                 final_g if last else None).reshape(bsz, t_len, d)
    return x
```

```python
import functools

import jax
import jax.numpy as jnp
from jax import lax
from jax.experimental import pallas as pl
from jax.experimental.pallas import tpu as pltpu

F32 = jnp.float32
BF16 = jnp.bfloat16

RMS_EPS = 1e-6
LN_EPS = 1e-5
LRU_C = 8.0
SB_HEAD_DIM = 64

V7X_LANES = 128
V7X_SUBLANES = 8
V7X_VMEM_LIMIT_BYTES = 56 * 1024 * 1024

FFN_ROWS = 1024
FFN_CHUNK = 512
SEQ_ROWS = 512
CONV_ROWS = 64
ATT_BLOCK = 256


def _cparams(*sem):
    return pltpu.CompilerParams(dimension_semantics=sem, vmem_limit_bytes=V7X_VMEM_LIMIT_BYTES)


def _resident(shape, lead=()):
    block = (None,) * len(lead) + tuple(shape[len(lead):])
    idx = tuple(lead) + (0,) * (len(shape) - len(lead))
    return pl.BlockSpec(block, lambda *_: idx, pipeline_mode=pl.Buffered(1))


def _rms(x, g):
    ms = jnp.mean(x * x, axis=-1, keepdims=True)
    return x * lax.rsqrt(ms + RMS_EPS) * g


def _dot(a, b):
    return jnp.dot(a, b, preferred_element_type=F32)


def _softplus(z):
    return jnp.maximum(z, 0.0) + jnp.log(1.0 + jnp.exp(-jnp.abs(z)))


def _neg_expm1(y):
    t = jnp.tanh(0.5 * y)
    small = -2.0 * t / (1.0 - t)
    return jnp.where(y > -0.5, small, 1.0 - jnp.exp(y))


def _ffn_kernel(x_ref, g_ref, win_ref, wout_ref, *rest, d_ff, final):
    if final:
        fg_ref, o_ref, act_ref = rest
    else:
        o_ref, act_ref = rest
    x = x_ref[...]
    xn = _rms(x, g_ref[...]).astype(BF16)
    for c0 in range(0, d_ff, FFN_CHUNK):
        cw = min(FFN_CHUNK, d_ff - c0)
        gate = _dot(xn, win_ref[:, c0:c0 + cw])
        up = _dot(xn, win_ref[:, d_ff + c0:d_ff + c0 + cw])
        act_ref[:, c0:c0 + cw] = (gate * jax.nn.sigmoid(gate) * up).astype(BF16)
    out = x + 0.5 * _dot(act_ref[...], wout_ref[...])
    if final:
        out = _rms(out, fg_ref[...])
    o_ref[...] = out


def _ffn(x2, norm_g, w_in, w_out, layer, which, final_g=None):
    n, d = x2.shape
    d_ff = w_out.shape[-2]
    tm = min(FFN_ROWS, n)
    final = final_g is not None
    row_spec = pl.BlockSpec((tm, d), lambda i: (i, 0))
    in_specs = [row_spec, _resident(norm_g.shape, (layer, 2 * which)), _resident(w_in.shape, (layer, which)),
                _resident(w_out.shape, (layer, which))]
    args = [x2, norm_g, w_in, w_out]
    if final:
        in_specs.append(_resident((1, d)))
        args.append(final_g.reshape(1, d))
    return pl.pallas_call(
        functools.partial(_ffn_kernel, d_ff=d_ff, final=final),
        grid=(n // tm,),
        in_specs=in_specs,
        out_specs=row_spec,
        out_shape=jax.ShapeDtypeStruct((n, d), F32),
        scratch_shapes=[pltpu.VMEM((tm, d_ff), BF16)],
        compiler_params=_cparams("arbitrary"),
        name="ffn_final" if final else "ffn",
    )(*args)


def _rglru_kernel(x_ref, g_ref, win_ref, cw_ref, cb_ref, gw_ref, gb_ref, lam_ref, wout_ref, o_ref,
                  xbuf, a_s, b_s, hcar, *, tt, d_rnn, conv_k):
    pair = 2 * V7X_LANES
    sub = V7X_SUBLANES

    @pl.when(pl.program_id(1) == 0)
    def _():
        xbuf[0:sub, :] = jnp.zeros((sub, d_rnn), F32)
        hcar[...] = jnp.zeros_like(hcar)

    x = x_ref[...]
    hn = _rms(x, g_ref[...]).astype(BF16)
    gate = _dot(hn, win_ref[:, :d_rnn])
    xr = _dot(hn, win_ref[:, d_rnn:])

    xbuf[sub:sub + tt, :] = xr
    xc = cb_ref[...] + cw_ref[conv_k - 1:conv_k, :] * xr
    for s in range(1, conv_k):
        xc = xc + cw_ref[conv_k - 1 - s:conv_k - s, :] * xbuf[sub - s:sub - s + tt, :]
    xbuf[0:sub, :] = xbuf[tt:tt + sub, :]

    xcb = xc.astype(BF16)
    sp_lam = _softplus(-lam_ref[...])
    for p in range(d_rnn // pair):
        cols = slice(p * pair, (p + 1) * pair)
        gp = _dot(xcb[:, cols], gw_ref[p])
        r_t = jax.nn.sigmoid(gp[:, :pair] + gb_ref[0:1, cols])
        i_t = jax.nn.sigmoid(gp[:, pair:] + gb_ref[1:2, cols])
        log_a = (-LRU_C) * r_t * sp_lam[:, cols]
        a_s[:, cols] = jnp.exp(log_a)
        b_s[:, cols] = jnp.sqrt(_neg_expm1(2.0 * log_a)) * (i_t * xc[:, cols])

    row = lax.broadcasted_iota(jnp.int32, (sub, d_rnn), 0)

    def group(j, hprev):
        r0 = pl.multiple_of(j * sub, sub)
        a = a_s[pl.ds(r0, sub), :]
        b = b_s[pl.ds(r0, sub), :]
        s = 1
        while s < sub:
            keep = row >= s
            a_sh = jnp.where(keep, pltpu.roll(a, s, 0), 1.0)
            b_sh = jnp.where(keep, pltpu.roll(b, s, 0), 0.0)
            b = a * b_sh + b
            a = a * a_sh
            s *= 2
        h = a * hprev + b
        b_s[pl.ds(r0, sub), :] = h
        return h[sub - 1:sub, :]

    hcar[...] = lax.fori_loop(0, tt // sub, group, hcar[...])

    y = (b_s[...] * jax.nn.gelu(gate)).astype(BF16)
    o_ref[...] = x + _dot(y, wout_ref[...])


def _rglru(x, norm_g, layer, j, w_in, conv_w, conv_b, gate_w_packed, gate_b, lam, w_out):
    bsz, t_len, d = x.shape
    d_rnn = w_out.shape[-2]
    conv_k = conv_w.shape[-2]
    tt = min(SEQ_ROWS, t_len)
    x_spec = pl.BlockSpec((None, tt, d), lambda b, t: (b, t, 0))
    args = [norm_g, w_in, conv_w, conv_b.reshape(-1, 1, d_rnn), gate_w_packed, gate_b.reshape(-1, 2, d_rnn),
            lam.reshape(-1, 1, d_rnn), w_out]
    in_specs = [x_spec, _resident(norm_g.shape, (layer, 1))] + [_resident(a.shape, (j,)) for a in args[1:]]
    return pl.pallas_call(
        functools.partial(_rglru_kernel, tt=tt, d_rnn=d_rnn, conv_k=conv_k),
        grid=(bsz, t_len // tt),
        in_specs=in_specs,
        out_specs=x_spec,
        out_shape=jax.ShapeDtypeStruct(x.shape, F32),
        scratch_shapes=[pltpu.VMEM((tt + V7X_SUBLANES, d_rnn), F32), pltpu.VMEM((tt, d_rnn), F32),
                        pltpu.VMEM((tt, d_rnn), F32), pltpu.VMEM((1, d_rnn), F32)],
        compiler_params=_cparams("arbitrary", "arbitrary"),
        name="rglru",
    )(x, *args)


def _pack_gate_w(gate_w):
    n, _, heads, blk, _ = gate_w.shape
    z = jnp.zeros((n, blk, blk), gate_w.dtype)
    out = []
    for p in range(heads // 2):
        h0, h1 = 2 * p, 2 * p + 1
        top = jnp.concatenate([gate_w[:, 0, h0], z, gate_w[:, 1, h0], z], axis=2)
        bot = jnp.concatenate([z, gate_w[:, 0, h1], z, gate_w[:, 1, h1]], axis=2)
        out.append(jnp.concatenate([top, bot], axis=1))
    return jnp.stack(out, axis=1).astype(BF16)


def _qkv_kernel(x_ref, g_ref, w_ref, q_ref, k_ref, v_ref, *, d, scale):
    hn = _rms(x_ref[...], g_ref[...]).astype(BF16)
    q_ref[...] = (_dot(hn, w_ref[:, :d]) * scale).astype(BF16)
    k_ref[...] = _dot(hn, w_ref[:, d:2 * d]).astype(BF16)
    v_ref[...] = _dot(hn, w_ref[:, 2 * d:]).astype(BF16)


def _qkv(x2, norm_g, layer, j, w_qkv):
    n, d = x2.shape
    tm = min(FFN_ROWS, n)
    row_spec = pl.BlockSpec((tm, d), lambda i: (i, 0))
    out = jax.ShapeDtypeStruct((n, d), BF16)
    return pl.pallas_call(
        functools.partial(_qkv_kernel, d=d, scale=SB_HEAD_DIM ** -0.5),
        grid=(n // tm,),
        in_specs=[row_spec, _resident(norm_g.shape, (layer, 1)), _resident(w_qkv.shape, (j,))],
        out_specs=[row_spec, row_spec, row_spec],
        out_shape=[out, out, out],
        compiler_params=_cparams("arbitrary"),
        name="qkv",
    )(x2, norm_g, w_qkv)


def _attn_kernel(q_ref, k_ref, v_ref, x_ref, wo_ref, o_ref, qs_s, later_s, acc_s, oacc_ref, *, blk, d):
    qi = pl.program_id(1)
    lanes = V7X_LANES
    n_pairs = d // lanes
    head0 = lax.broadcasted_iota(jnp.int32, (1, lanes), 1) < SB_HEAD_DIM
    rows = lax.broadcasted_iota(jnp.int32, (2 * blk, blk), 0)
    rows = jnp.where(rows >= blk, rows - blk, rows)
    diag_mask = lax.broadcasted_iota(jnp.int32, (2 * blk, blk), 1) < rows
    jj = lax.broadcasted_iota(jnp.int32, (lanes, lanes), 0)
    kk = lax.broadcasted_iota(jnp.int32, (lanes, lanes), 1)
    tri = jnp.concatenate([(jj >= kk).astype(BF16), jnp.ones((lanes, lanes), BF16)], axis=1)
    tri = jnp.concatenate([tri, tri], axis=0)

    for p in range(n_pairs):
        qp = q_ref[:, p * lanes:(p + 1) * lanes]
        zero = jnp.zeros_like(qp)
        qs_s[p] = jnp.concatenate([jnp.where(head0, qp, zero), jnp.where(head0, zero, qp)], axis=0)

    def chain(p, kb, first):
        cs = slice(p * lanes, (p + 1) * lanes)
        k0 = pl.multiple_of(kb * blk, blk)
        kp = k_ref[pl.ds(k0, blk), cs]
        vp = v_ref[pl.ds(k0, blk), cs]
        z = lax.dot_general(qs_s[p], kp, (((1,), (1,)), ((), ())), preferred_element_type=F32)
        sp = _softplus(z)
        if first:
            sp = jnp.where(diag_mask, sp, 0.0)
        hi_f = lax.bitcast_convert_type(lax.bitcast_convert_type(sp, jnp.uint32) & jnp.uint32(0xFFFF0000), F32)
        hi = hi_f.astype(BF16)
        lo = (sp - hi_f).astype(BF16)
        sums = []
        later = None if first else later_s[p]
        for h0 in range(blk - lanes, -1, -lanes):
            out = _dot(jnp.concatenate([hi[:, h0:h0 + lanes], lo[:, h0:h0 + lanes]], axis=1), tri)
            sums.append(out[:, :lanes] if later is None else out[:, :lanes] + later)
            later = out[:, lanes:] if later is None else later + out[:, lanes:]
        w = jnp.exp(z - jnp.concatenate(sums[::-1], axis=1))
        if first:
            w = jnp.where(diag_mask, w, 0.0)
        pv = _dot(w.astype(BF16), vp)
        later_s[p] = later
        acc_s[p] = pv if first else acc_s[p] + pv

    for p in range(n_pairs):
        chain(p, qi, True)

    def older(j, carry):
        for p in range(n_pairs):
            chain(p, qi - 1 - j, False)
        return carry

    lax.fori_loop(0, qi, older, 0)

    for p in range(n_pairs):
        acc = acc_s[p]
        oacc_ref[:, p * lanes:(p + 1) * lanes] = jnp.where(head0, acc[:blk], acc[blk:]).astype(BF16)
    o_ref[...] = x_ref[...] + _dot(oacc_ref[...], wo_ref[...])


def _attn(x, q, k, v, j, w_o):
    bsz, t_len, d = x.shape
    blk = min(ATT_BLOCK, t_len)
    n_pairs = d // V7X_LANES
    q_spec = pl.BlockSpec((None, blk, d), lambda b, i: (b, i, 0))
    kv_spec = pl.BlockSpec((None, t_len, d), lambda b, i: (b, 0, 0))
    return pl.pallas_call(
        functools.partial(_attn_kernel, blk=blk, d=d),
        grid=(bsz, t_len // blk),
        in_specs=[q_spec, kv_spec, kv_spec, q_spec, _resident(w_o.shape, (j,))],
        out_specs=q_spec,
        out_shape=jax.ShapeDtypeStruct(x.shape, F32),
        scratch_shapes=[pltpu.VMEM((n_pairs, 2 * blk, V7X_LANES), BF16),
                        pltpu.VMEM((n_pairs, 2 * blk, V7X_LANES), F32),
                        pltpu.VMEM((n_pairs, 2 * blk, V7X_LANES), F32),
                        pltpu.VMEM((blk, d), BF16)],
        compiler_params=_cparams("arbitrary", "arbitrary"),
        name="sb_attn",
    )(q, k, v, x, w_o)


def _conformer_kernel(x_ref, g_ref, win_ref, bin_ref, cw_ref, cb_ref, lng_ref, lnb_ref, wout_ref, bout_ref,
                      o_ref, shifted, conv_out, *, tt, d, conv_k, halo):
    sub = V7X_SUBLANES
    lanes = V7X_LANES

    @pl.when(pl.program_id(1) == 0)
    def _():
        shifted[0, 0:halo, :] = jnp.zeros((halo, d), F32)

    x = x_ref[...]
    hn = _rms(x, g_ref[...]).astype(BF16)
    val = _dot(hn, win_ref[:, :d]) + bin_ref[:, :d]
    gt = _dot(hn, win_ref[:, d:]) + bin_ref[:, d:]
    shifted[0, halo:halo + tt, :] = val * jax.nn.sigmoid(gt)
    for r in range(1, sub):
        shifted[r, sub:halo + tt, :] = shifted[0, sub - r:halo + tt - r, :]

    for lc in range(d // lanes):
        cs = slice(lc * lanes, (lc + 1) * lanes)

        def chunk(c, _, cs=cs):
            r0 = pl.multiple_of(c * CONV_ROWS, CONV_ROWS)
            acc = jnp.broadcast_to(cb_ref[:, cs], (CONV_ROWS, lanes))
            for s in range(conv_k):
                a, r = divmod(s, sub)
                tap = cw_ref[conv_k - 1 - s:conv_k - s, cs]
                acc = acc + tap * shifted[r, pl.ds(r0 + (halo - sub * a), CONV_ROWS), cs]
            conv_out[pl.ds(r0, CONV_ROWS), cs] = acc
            return 0

        lax.fori_loop(0, tt // CONV_ROWS, chunk, 0)

    shifted[0, 0:halo, :] = shifted[0, tt:tt + halo, :]

    c = conv_out[...]
    mu = jnp.mean(c, axis=-1, keepdims=True)
    cc = c - mu
    var = jnp.mean(cc * cc, axis=-1, keepdims=True)
    y = cc * lax.rsqrt(var + LN_EPS) * lng_ref[...] + lnb_ref[...]
    y = (y * jax.nn.sigmoid(y)).astype(BF16)
    o_ref[...] = x + _dot(y, wout_ref[...]) + bout_ref[...]


def _conformer(x, norm_g, layer, j, w_in, b_in, conv_w, conv_b, ln_g, ln_b, w_out, b_out):
    bsz, t_len, d = x.shape
    conv_k = conv_w.shape[-2]
    halo = -(-(conv_k - 1) // V7X_SUBLANES) * V7X_SUBLANES
    tt = min(SEQ_ROWS // 2, t_len)
    x_spec = pl.BlockSpec((None, tt, d), lambda b, t: (b, t, 0))
    args = [norm_g, w_in, b_in.reshape(-1, 1, 2 * d), conv_w, conv_b.reshape(-1, 1, d), ln_g.reshape(-1, 1, d),
            ln_b.reshape(-1, 1, d), w_out, b_out.reshape(-1, 1, d)]
    in_specs = [x_spec, _resident(norm_g.shape, (layer, 1))] + [_resident(a.shape, (j,)) for a in args[1:]]
    return pl.pallas_call(
        functools.partial(_conformer_kernel, tt=tt, d=d, conv_k=conv_k, halo=halo),
        grid=(bsz, t_len // tt),
        in_specs=in_specs,
        out_specs=x_spec,
        out_shape=jax.ShapeDtypeStruct(x.shape, F32),
        scratch_shapes=[pltpu.VMEM((V7X_SUBLANES, tt + halo, d), F32), pltpu.VMEM((tt, d), F32)],
        compiler_params=_cparams("arbitrary", "arbitrary"),
        name="conformer",
    )(x, *args)


def kernel(x, norm_g, ffn_w_in, ffn_w_out, a_w_in, a_conv_w, a_conv_b, a_gate_w, a_gate_b, a_lambda, a_w_out,
           b_w_qkv, b_w_o, c_w_in, c_b_in, c_conv_w, c_conv_b, c_ln_g, c_ln_b, c_w_out, c_b_out, final_g):
    bsz, t_len, d = x.shape
    depth = norm_g.shape[0]
    n_mixers = 3
    norm_g = norm_g.reshape(depth, 3, 1, d)
    ffn_w_in = ffn_w_in.astype(BF16)
    ffn_w_out = ffn_w_out.astype(BF16)
    a_w_in, a_w_out, a_gate_w = a_w_in.astype(BF16), a_w_out.astype(BF16), _pack_gate_w(a_gate_w)
    b_w_qkv, b_w_o = b_w_qkv.astype(BF16), b_w_o.astype(BF16)
    c_w_in, c_w_out = c_w_in.astype(BF16), c_w_out.astype(BF16)

    def flat(z):
        return z.reshape(bsz * t_len, d)

    for layer in range(depth):
        kind, j = layer % n_mixers, layer // n_mixers
        x = _ffn(flat(x), norm_g, ffn_w_in, ffn_w_out, layer, 0).reshape(bsz, t_len, d)
        if kind == 0:
            x = _rglru(x, norm_g, layer, j, a_w_in, a_conv_w, a_conv_b, a_gate_w, a_gate_b, a_lambda, a_w_out)
        elif kind == 1:
            q, k, v = _qkv(flat(x), norm_g, layer, j, b_w_qkv)
            shape = (bsz, t_len, d)
            x = _attn(x, q.reshape(shape), k.reshape(shape), v.reshape(shape), j, b_w_o)
        else:
            x = _conformer(x, norm_g, layer, j, c_w_in, c_b_in, c_conv_w, c_conv_b, c_ln_g, c_ln_b, c_w_out,
                           c_b_out)
        last = layer == depth - 1
        x = _ffn(flat(x), norm_g, ffn_w_in, ffn_w_out, layer, 1, final_g if last else None).reshape(bsz, t_len, d)
    return x
```

```python
import functools

import jax
import jax.numpy as jnp
from jax import lax
from jax.experimental import pallas as pl
from jax.experimental.pallas import tpu as pltpu

F32 = jnp.float32
BF16 = jnp.bfloat16

RMS_EPS = 1e-6
LN_EPS = 1e-5
LRU_C = 8.0
SB_HEAD_DIM = 64

V7X_LANES = 128
V7X_SUBLANES = 8
V7X_VMEM_LIMIT_BYTES = 56 * 1024 * 1024

FFN_ROWS = 1024
FFN_CHUNK = 512
SEQ_ROWS = 512
CONV_ROWS = 64
SQRT_FLOOR = 1e-37
ATT_BLOCK = 256


def _cparams(*sem):
    return pltpu.CompilerParams(dimension_semantics=sem, vmem_limit_bytes=V7X_VMEM_LIMIT_BYTES)


def _resident(shape, lead=()):
    block = (None,) * len(lead) + tuple(shape[len(lead):])
    idx = tuple(lead) + (0,) * (len(shape) - len(lead))
    return pl.BlockSpec(block, lambda *_: idx, pipeline_mode=pl.Buffered(1))


def _rms(x, g):
    ms = jnp.mean(x * x, axis=-1, keepdims=True)
    return x * lax.rsqrt(ms + RMS_EPS) * g


def _dot(a, b):
    return jnp.dot(a, b, preferred_element_type=F32)


def _softplus(z):
    return jnp.maximum(z, 0.0) + jnp.log(1.0 + jnp.exp(-jnp.abs(z)))


def _half_tanh_sigmoid(half_x):
    return 0.5 * jnp.tanh(half_x) + 0.5


def _gelu_tanh(x):
    c0 = 0.7978845608028654
    inner = x * (c0 + (c0 * 0.044715) * (x * x))
    return (0.5 * x) * (1.0 + jnp.tanh(inner))


def _x_spec(time_major, tm, d):
    if time_major:
        return pl.BlockSpec((tm, d), lambda b, t: (t, b))
    return pl.BlockSpec((None, tm, d), lambda b, t: (b, t, 0))


def _ffn_kernel(x_ref, g_ref, win_ref, wout_ref, *rest, d_ff, final):
    if final:
        fg_ref, o_ref, act_ref = rest
    else:
        o_ref, act_ref = rest
    x = x_ref[...]
    xn = _rms(x, g_ref[...]).astype(BF16)
    for c0 in range(0, d_ff, FFN_CHUNK):
        cw = min(FFN_CHUNK, d_ff - c0)
        gate = _dot(xn, win_ref[:, c0:c0 + cw])
        up = _dot(xn, win_ref[:, d_ff + c0:d_ff + c0 + cw])
        act_ref[:, c0:c0 + cw] = (gate * jax.nn.sigmoid(gate) * up).astype(BF16)
    out = x + 0.5 * _dot(act_ref[...], wout_ref[...])
    if final:
        out = _rms(out, fg_ref[...])
    o_ref[...] = out


def _ffn(x, norm_g, w_in, w_out, layer, which, in_tm, out_tm, final_g=None):
    t_len, bsz, d = x.shape if in_tm else (x.shape[1], x.shape[0], x.shape[2])
    d_ff = w_out.shape[-2]
    tm = min(FFN_ROWS, t_len)
    final = final_g is not None
    in_specs = [_x_spec(in_tm, tm, d), _resident(norm_g.shape, (layer, 2 * which)),
                _resident(w_in.shape, (layer, which)), _resident(w_out.shape, (layer, which))]
    args = [x.reshape(t_len, bsz * d) if in_tm else x, norm_g, w_in, w_out]
    if final:
        in_specs.append(_resident((1, d)))
        args.append(final_g.reshape(1, d))
    out = pl.pallas_call(
        functools.partial(_ffn_kernel, d_ff=d_ff, final=final),
        grid=(bsz, t_len // tm),
        in_specs=in_specs,
        out_specs=_x_spec(out_tm, tm, d),
        out_shape=jax.ShapeDtypeStruct((t_len, bsz * d) if out_tm else (bsz, t_len, d), F32),
        scratch_shapes=[pltpu.VMEM((tm, d_ff), BF16)],
        compiler_params=_cparams("arbitrary", "arbitrary"),
        name="ffn_final" if final else "ffn",
    )(*args)
    return out.reshape(t_len, bsz, d) if out_tm else out


def _rglru_kernel(x_ref, g_ref, win_ref, cw_ref, cb_ref, gw_ref, gb_ref, lam_ref, wout_ref, o_ref,
                  xbuf, a_s, b_s, hcar, *, rows, bsz, d_rnn, conv_k):
    pair = 2 * V7X_LANES
    halo = (conv_k - 1) * bsz

    @pl.when(pl.program_id(0) == 0)
    def _():
        xbuf[0:halo, :] = jnp.zeros((halo, d_rnn), F32)
        hcar[...] = jnp.zeros_like(hcar)

    x = x_ref[...]
    hn = _rms(x, g_ref[...]).astype(BF16)
    gate = _dot(hn, win_ref[:, :d_rnn])
    xr = _dot(hn, win_ref[:, d_rnn:])

    xbuf[halo:halo + rows, :] = xr
    xc = cb_ref[...] + cw_ref[conv_k - 1:conv_k, :] * xr
    for s in range(1, conv_k):
        xc = xc + cw_ref[conv_k - 1 - s:conv_k - s, :] * xbuf[halo - s * bsz:halo - s * bsz + rows, :]
    xbuf[0:halo, :] = xbuf[rows:rows + halo, :]

    xcb = xc.astype(BF16)
    c1 = (-0.5 * LRU_C) * _softplus(-lam_ref[...])
    for p in range(d_rnn // pair):
        cols = slice(p * pair, (p + 1) * pair)
        gp = _dot(xcb[:, cols], gw_ref[p])
        t_r = jnp.tanh(gp[:, :pair] + gb_ref[0:1, cols])
        i_t = _half_tanh_sigmoid(gp[:, pair:] + gb_ref[1:2, cols])
        log_a = c1[:, cols] * t_r + c1[:, cols]
        a = jnp.exp(log_a)
        u = jnp.tanh(log_a) * (-1.0 - a * a)
        root = u * lax.rsqrt(jnp.maximum(u, SQRT_FLOOR))
        a_s[:, cols] = a
        b_s[:, cols] = (root * xc[:, cols]) * i_t

    def step(t, h):
        r0 = pl.multiple_of(t * bsz, bsz)
        h = a_s[pl.ds(r0, bsz), :] * h + b_s[pl.ds(r0, bsz), :]
        b_s[pl.ds(r0, bsz), :] = h
        return h

    hcar[...] = lax.fori_loop(0, rows // bsz, step, hcar[...], unroll=8)

    y = (b_s[...] * _gelu_tanh(gate)).astype(BF16)
    o_ref[...] = x + _dot(y, wout_ref[...])


def _rglru(x_tm, norm_g, layer, j, w_in, conv_w, conv_b, gate_w_half, gate_b_half, lam, w_out):
    t_len, bsz, d = x_tm.shape
    assert bsz == V7X_SUBLANES, "one time step must fill one sublane group"
    d_rnn = w_out.shape[-2]
    conv_k = conv_w.shape[-2]
    rows = min(SEQ_ROWS, t_len * bsz)
    x_spec = pl.BlockSpec((rows, d), lambda t: (t, 0))
    args = [norm_g, w_in, conv_w, conv_b.reshape(-1, 1, d_rnn), gate_w_half, gate_b_half.reshape(-1, 2, d_rnn),
            lam.reshape(-1, 1, d_rnn), w_out]
    in_specs = [x_spec, _resident(norm_g.shape, (layer, 1))] + [_resident(a.shape, (j,)) for a in args[1:]]
    out = pl.pallas_call(
        functools.partial(_rglru_kernel, rows=rows, bsz=bsz, d_rnn=d_rnn, conv_k=conv_k),
        grid=(t_len * bsz // rows,),
        in_specs=in_specs,
        out_specs=x_spec,
        out_shape=jax.ShapeDtypeStruct((t_len * bsz, d), F32),
        scratch_shapes=[pltpu.VMEM((rows + (conv_k - 1) * bsz, d_rnn), F32), pltpu.VMEM((rows, d_rnn), F32),
                        pltpu.VMEM((rows, d_rnn), F32), pltpu.VMEM((bsz, d_rnn), F32)],
        compiler_params=_cparams("arbitrary"),
        name="rglru",
    )(x_tm.reshape(t_len * bsz, d), *args)
    return out.reshape(t_len, bsz, d)


def _pack_gate_w(gate_w):
    n, _, heads, blk, _ = gate_w.shape
    z = jnp.zeros((n, blk, blk), gate_w.dtype)
    out = []
    for p in range(heads // 2):
        h0, h1 = 2 * p, 2 * p + 1
        top = jnp.concatenate([gate_w[:, 0, h0], z, gate_w[:, 1, h0], z], axis=2)
        bot = jnp.concatenate([z, gate_w[:, 0, h1], z, gate_w[:, 1, h1]], axis=2)
        out.append(jnp.concatenate([top, bot], axis=1))
    return (0.5 * jnp.stack(out, axis=1)).astype(BF16)


def _qkv_kernel(x_ref, g_ref, w_ref, q_ref, k_ref, v_ref, *, d, scale):
    hn = _rms(x_ref[...], g_ref[...]).astype(BF16)
    q_ref[...] = (_dot(hn, w_ref[:, :d]) * scale).astype(BF16)
    k_ref[...] = _dot(hn, w_ref[:, d:2 * d]).astype(BF16)
    v_ref[...] = _dot(hn, w_ref[:, 2 * d:]).astype(BF16)


def _qkv(x, norm_g, layer, j, w_qkv):
    bsz, t_len, d = x.shape
    tm = min(FFN_ROWS, t_len)
    row_spec = _x_spec(False, tm, d)
    out = jax.ShapeDtypeStruct(x.shape, BF16)
    return pl.pallas_call(
        functools.partial(_qkv_kernel, d=d, scale=SB_HEAD_DIM ** -0.5),
        grid=(bsz, t_len // tm),
        in_specs=[row_spec, _resident(norm_g.shape, (layer, 1)), _resident(w_qkv.shape, (j,))],
        out_specs=[row_spec, row_spec, row_spec],
        out_shape=[out, out, out],
        compiler_params=_cparams("arbitrary", "arbitrary"),
        name="qkv",
    )(x, norm_g, w_qkv)


def _attn_kernel(q_ref, k_ref, v_ref, x_ref, wo_ref, o_ref, qs_s, later_s, acc_s, oacc_ref, *, blk, d):
    qi = pl.program_id(1)
    lanes = V7X_LANES
    n_pairs = d // lanes
    head0 = lax.broadcasted_iota(jnp.int32, (1, lanes), 1) < SB_HEAD_DIM
    rows = lax.broadcasted_iota(jnp.int32, (2 * blk, blk), 0)
    rows = jnp.where(rows >= blk, rows - blk, rows)
    diag_mask = lax.broadcasted_iota(jnp.int32, (2 * blk, blk), 1) < rows
    jj = lax.broadcasted_iota(jnp.int32, (lanes, lanes), 0)
    kk = lax.broadcasted_iota(jnp.int32, (lanes, lanes), 1)
    tri = jnp.concatenate([(jj >= kk).astype(BF16), jnp.ones((lanes, lanes), BF16)], axis=1)
    tri = jnp.concatenate([tri, tri], axis=0)

    for p in range(n_pairs):
        qp = q_ref[:, p * lanes:(p + 1) * lanes]
        zero = jnp.zeros_like(qp)
        qs_s[p] = jnp.concatenate([jnp.where(head0, qp, zero), jnp.where(head0, zero, qp)], axis=0)

    def chain(p, kb, first):
        cs = slice(p * lanes, (p + 1) * lanes)
        k0 = pl.multiple_of(kb * blk, blk)
        kp = k_ref[pl.ds(k0, blk), cs]
        vp = v_ref[pl.ds(k0, blk), cs]
        z = lax.dot_general(qs_s[p], kp, (((1,), (1,)), ((), ())), preferred_element_type=F32)
        sp = _softplus(z)
        if first:
            sp = jnp.where(diag_mask, sp, 0.0)
        hi_f = lax.bitcast_convert_type(lax.bitcast_convert_type(sp, jnp.uint32) & jnp.uint32(0xFFFF0000), F32)
        hi = hi_f.astype(BF16)
        lo = (sp - hi_f).astype(BF16)
        sums = []
        later = None if first else later_s[p]
        for h0 in range(blk - lanes, -1, -lanes):
            out = _dot(jnp.concatenate([hi[:, h0:h0 + lanes], lo[:, h0:h0 + lanes]], axis=1), tri)
            sums.append(out[:, :lanes] if later is None else out[:, :lanes] + later)
            later = out[:, lanes:] if later is None else later + out[:, lanes:]
        w = jnp.exp(z - jnp.concatenate(sums[::-1], axis=1))
        if first:
            w = jnp.where(diag_mask, w, 0.0)
        pv = _dot(w.astype(BF16), vp)
        later_s[p] = later
        acc_s[p] = pv if first else acc_s[p] + pv

    for p in range(n_pairs):
        chain(p, qi, True)

    def older(j, carry):
        for p in range(n_pairs):
            chain(p, qi - 1 - j, False)
        return carry

    lax.fori_loop(0, qi, older, 0)

    for p in range(n_pairs):
        acc = acc_s[p]
        oacc_ref[:, p * lanes:(p + 1) * lanes] = jnp.where(head0, acc[:blk], acc[blk:]).astype(BF16)
    o_ref[...] = x_ref[...] + _dot(oacc_ref[...], wo_ref[...])


def _attn(x, q, k, v, j, w_o):
    bsz, t_len, d = x.shape
    blk = min(ATT_BLOCK, t_len)
    n_pairs = d // V7X_LANES
    q_spec = pl.BlockSpec((None, blk, d), lambda b, i: (b, i, 0))
    kv_spec = pl.BlockSpec((None, t_len, d), lambda b, i: (b, 0, 0))
    return pl.pallas_call(
        functools.partial(_attn_kernel, blk=blk, d=d),
        grid=(bsz, t_len // blk),
        in_specs=[q_spec, kv_spec, kv_spec, q_spec, _resident(w_o.shape, (j,))],
        out_specs=q_spec,
        out_shape=jax.ShapeDtypeStruct(x.shape, F32),
        scratch_shapes=[pltpu.VMEM((n_pairs, 2 * blk, V7X_LANES), BF16),
                        pltpu.VMEM((n_pairs, 2 * blk, V7X_LANES), F32),
                        pltpu.VMEM((n_pairs, 2 * blk, V7X_LANES), F32),
                        pltpu.VMEM((blk, d), BF16)],
        compiler_params=_cparams("arbitrary", "arbitrary"),
        name="sb_attn",
    )(q, k, v, x, w_o)


def _conformer_kernel(x_ref, g_ref, win_ref, bin_ref, cw_ref, cb_ref, lng_ref, lnb_ref, wout_ref, bout_ref,
                      o_ref, hbuf, conv_out, *, rows, bsz, d, conv_k):
    lanes = V7X_LANES
    n_lc = d // lanes
    halo = (conv_k - 1) * bsz

    @pl.when(pl.program_id(0) == 0)
    def _():
        hbuf[:, 0:halo, :] = jnp.zeros((n_lc, halo, lanes), F32)

    x = x_ref[...]
    hn = _rms(x, g_ref[...]).astype(BF16)
    val = _dot(hn, win_ref[:, :d]) + bin_ref[:, :d]
    gt = _dot(hn, win_ref[:, d:]) + bin_ref[:, d:]
    h = val * _half_tanh_sigmoid(0.5 * gt)
    for lc in range(n_lc):
        hbuf[lc, halo:halo + rows, :] = h[:, lc * lanes:(lc + 1) * lanes]

    for lc in range(n_lc):
        cs = slice(lc * lanes, (lc + 1) * lanes)

        def chunk(c, _, lc=lc, cs=cs):
            r0 = pl.multiple_of(c * CONV_ROWS, CONV_ROWS)
            acc = jnp.broadcast_to(cb_ref[:, cs], (CONV_ROWS, lanes))
            for s in range(conv_k):
                tap = cw_ref[conv_k - 1 - s:conv_k - s, cs]
                acc = acc + tap * hbuf[lc, pl.ds(r0 + (halo - s * bsz), CONV_ROWS), :]
            conv_out[lc, pl.ds(r0, CONV_ROWS), :] = acc
            return 0

        lax.fori_loop(0, rows // CONV_ROWS, chunk, 0)

    hbuf[:, 0:halo, :] = hbuf[:, rows:rows + halo, :]

    c = jnp.concatenate([conv_out[lc] for lc in range(n_lc)], axis=1)
    mu = jnp.mean(c, axis=-1, keepdims=True)
    cc = c - mu
    var = jnp.mean(cc * cc, axis=-1, keepdims=True)
    y = cc * lax.rsqrt(var + LN_EPS) * lng_ref[...] + lnb_ref[...]
    y = (y * _half_tanh_sigmoid(0.5 * y)).astype(BF16)
    o_ref[...] = x + _dot(y, wout_ref[...]) + bout_ref[...]


def _conformer(x_tm, norm_g, layer, j, w_in, b_in, conv_w, conv_b, ln_g, ln_b, w_out, b_out):
    t_len, bsz, d = x_tm.shape
    assert bsz == V7X_SUBLANES, "one time step must fill one sublane group"
    conv_k = conv_w.shape[-2]
    rows = min(SEQ_ROWS, t_len * bsz)
    assert (conv_k - 1) * bsz <= rows
    x_spec = pl.BlockSpec((rows, d), lambda t: (t, 0))
    args = [norm_g, w_in, b_in.reshape(-1, 1, 2 * d), conv_w, conv_b.reshape(-1, 1, d), ln_g.reshape(-1, 1, d),
            ln_b.reshape(-1, 1, d), w_out, b_out.reshape(-1, 1, d)]
    in_specs = [x_spec, _resident(norm_g.shape, (layer, 1))] + [_resident(a.shape, (j,)) for a in args[1:]]
    out = pl.pallas_call(
        functools.partial(_conformer_kernel, rows=rows, bsz=bsz, d=d, conv_k=conv_k),
        grid=(t_len * bsz // rows,),
        in_specs=in_specs,
        out_specs=x_spec,
        out_shape=jax.ShapeDtypeStruct((t_len * bsz, d), F32),
        scratch_shapes=[pltpu.VMEM((d // V7X_LANES, rows + (conv_k - 1) * bsz, V7X_LANES), F32),
                        pltpu.VMEM((d // V7X_LANES, rows, V7X_LANES), F32)],
        compiler_params=_cparams("arbitrary"),
        name="conformer",
    )(x_tm.reshape(t_len * bsz, d), *args)
    return out.reshape(t_len, bsz, d)


def kernel(x, norm_g, ffn_w_in, ffn_w_out, a_w_in, a_conv_w, a_conv_b, a_gate_w, a_gate_b, a_lambda, a_w_out,
           b_w_qkv, b_w_o, c_w_in, c_b_in, c_conv_w, c_conv_b, c_ln_g, c_ln_b, c_w_out, c_b_out, final_g):
    bsz, t_len, d = x.shape
    depth = norm_g.shape[0]
    n_mixers = 3
    attention = 1
    norm_g = norm_g.reshape(depth, 3, 1, d)
    ffn_w_in = ffn_w_in.astype(BF16)
    ffn_w_out = ffn_w_out.astype(BF16)
    a_w_in, a_w_out = a_w_in.astype(BF16), a_w_out.astype(BF16)
    a_gate_w, a_gate_b = _pack_gate_w(a_gate_w), 0.5 * a_gate_b
    b_w_qkv, b_w_o = b_w_qkv.astype(BF16), b_w_o.astype(BF16)
    c_w_in, c_w_out = c_w_in.astype(BF16), c_w_out.astype(BF16)

    time_major = False
    for layer in range(depth):
        kind, j = layer % n_mixers, layer // n_mixers
        mixer_tm = kind != attention
        x = _ffn(x, norm_g, ffn_w_in, ffn_w_out, layer, 0, time_major, mixer_tm)
        if kind == 0:
            x = _rglru(x, norm_g, layer, j, a_w_in, a_conv_w, a_conv_b, a_gate_w, a_gate_b, a_lambda, a_w_out)
        elif kind == attention:
            q, k, v = _qkv(x, norm_g, layer, j, b_w_qkv)
            x = _attn(x, q, k, v, j, b_w_o)
        else:
            x = _conformer(x, norm_g, layer, j, c_w_in, c_b_in, c_conv_w, c_conv_b, c_ln_g, c_ln_b, c_w_out,
                           c_b_out)
        last = layer == depth - 1
        time_major = (not last) and (layer + 1) % n_mixers != attention
        x = _ffn(x, norm_g, ffn_w_in, ffn_w_out, layer, 1, mixer_tm, time_major, final_g if last else None)
    return x
```

```python
import functools

import jax
import jax.numpy as jnp
from jax import lax
from jax.experimental import pallas as pl
from jax.experimental.pallas import tpu as pltpu

F32 = jnp.float32
BF16 = jnp.bfloat16

RMS_EPS = 1e-6
LN_EPS = 1e-5
LRU_C = 8.0
SB_HEAD_DIM = 64
LOG2_E = 1.4426950408889634

V7X_LANES = 128
V7X_SUBLANES = 8
V7X_VMEM_LIMIT_BYTES = 56 * 1024 * 1024

FFN_ROWS = 1024
FFN_CHUNK = 512
SEQ_ROWS = 512
CONV_ROWS = 64
SQRT_FLOOR = 1e-37
ATT_BLOCK = 256


def _cparams(*sem):
    return pltpu.CompilerParams(dimension_semantics=sem, vmem_limit_bytes=V7X_VMEM_LIMIT_BYTES)


def _resident(shape, lead=()):
    block = (None,) * len(lead) + tuple(shape[len(lead):])
    idx = tuple(lead) + (0,) * (len(shape) - len(lead))
    return pl.BlockSpec(block, lambda *_: idx, pipeline_mode=pl.Buffered(1))


def _rms(x, g):
    ms = jnp.mean(x * x, axis=-1, keepdims=True)
    return x * lax.rsqrt(ms + RMS_EPS) * g


def _dot(a, b):
    return jnp.dot(a, b, preferred_element_type=F32)


def _softplus(z):
    return jnp.maximum(z, 0.0) + jnp.log(1.0 + jnp.exp2(jnp.abs(z) * (-LOG2_E)))


def _half_tanh_sigmoid(half_x):
    return 0.5 * jnp.tanh(half_x) + 0.5


def _gelu_tanh(x):
    c0 = 0.7978845608028654
    inner = x * (c0 + (c0 * 0.044715) * (x * x))
    return (0.5 * x) * (1.0 + jnp.tanh(inner))


def _row_spec(time_major, swap, tm, bsz, t_len, d):
    if time_major:
        return pl.BlockSpec((tm, d), lambda i: (i, 0))
    if swap:
        return pl.BlockSpec((bsz, tm // bsz, d), lambda i: (0, i, 0))
    per_batch = t_len // tm
    return pl.BlockSpec((None, tm, d), lambda i: (i // per_batch, i % per_batch, 0))


def _ffn_kernel(*refs, d_ff, final, in_swap, out_swap, convert_next):
    refs = list(refs)
    x_ref, g_ref, win_ref, wout_ref = refs[:4]
    del refs[:4]
    fg_ref = refs.pop(0) if final else None
    if convert_next:
        nwin_ref, nwout_ref, o_ref, owin_ref, owout_ref, act_ref = refs
        owin_ref[...] = nwin_ref[...].astype(BF16)
        owout_ref[...] = nwout_ref[...].astype(BF16)
    else:
        o_ref, act_ref = refs
    x = x_ref[...]
    if in_swap:
        x = jnp.swapaxes(x, 0, 1).reshape(-1, x.shape[-1])
    xn = _rms(x, g_ref[...]).astype(BF16)
    for c0 in range(0, d_ff, FFN_CHUNK):
        cw = min(FFN_CHUNK, d_ff - c0)
        gate = _dot(xn, win_ref[:, c0:c0 + cw])
        up = _dot(xn, win_ref[:, d_ff + c0:d_ff + c0 + cw])
        act_ref[:, c0:c0 + cw] = (gate * jax.nn.sigmoid(gate) * up).astype(BF16)
    out = x + 0.5 * _dot(act_ref[...], wout_ref[...])
    if final:
        out = _rms(out, fg_ref[...])
    if out_swap:
        bsz = o_ref.shape[0]
        out = jnp.swapaxes(out.reshape(-1, bsz, out.shape[-1]), 0, 1)
    o_ref[...] = out


def _ffn(x, norm_g, w_in, w_out, layer, which, bsz, in_tm, out_tm, final_g=None, next_w=None):
    d = x.shape[-1]
    t_len = x.shape[0] // bsz if in_tm else x.shape[1]
    d_ff = w_out.shape[-2]
    tm = min(FFN_ROWS, t_len)
    steps = bsz * t_len // tm
    final = final_g is not None
    in_swap, out_swap = (not in_tm) and out_tm, in_tm and not out_tm
    in_specs = [_row_spec(in_tm, in_swap, tm, bsz, t_len, d), _resident(norm_g.shape, (layer, 2 * which)),
                _resident(w_in.shape), _resident(w_out.shape)]
    args = [x, norm_g, w_in, w_out]
    out_specs = [_row_spec(out_tm, out_swap, tm, bsz, t_len, d)]
    out_shape = [jax.ShapeDtypeStruct((t_len * bsz, d) if out_tm else (bsz, t_len, d), F32)]
    if final:
        in_specs.append(_resident((1, d)))
        args.append(final_g.reshape(1, d))
    if next_w is not None:
        nw_in, nw_out, nl, nwhich = next_w
        for w_all in (nw_in, nw_out):
            r, c = w_all.shape[-2:]
            assert r % (steps * 2 * V7X_SUBLANES) == 0
            in_specs.append(pl.BlockSpec((None, None, r // steps, c), lambda i, nl=nl, nwhich=nwhich: (nl, nwhich, i, 0)))
            args.append(w_all)
            out_specs.append(pl.BlockSpec((r // steps, c), lambda i: (i, 0)))
            out_shape.append(jax.ShapeDtypeStruct((r, c), BF16))
    return pl.pallas_call(
        functools.partial(_ffn_kernel, d_ff=d_ff, final=final, in_swap=in_swap, out_swap=out_swap,
                          convert_next=next_w is not None),
        grid=(steps,),
        in_specs=in_specs,
        out_specs=out_specs,
        out_shape=out_shape,
        scratch_shapes=[pltpu.VMEM((tm, d_ff), BF16)],
        compiler_params=_cparams("arbitrary"),
        name="ffn_final" if final else "ffn",
    )(*args)


def _rglru_kernel(x_ref, g_ref, win_ref, cw_ref, cb_ref, gw_ref, gb_ref, lam_ref, wout_ref, o_ref,
                  xbuf, a_s, b_s, hcar, *, rows, bsz, d_rnn, conv_k):
    pair = 2 * V7X_LANES
    halo = (conv_k - 1) * bsz

    @pl.when(pl.program_id(0) == 0)
    def _():
        xbuf[0:halo, :] = jnp.zeros((halo, d_rnn), F32)
        hcar[...] = jnp.zeros_like(hcar)

    x = x_ref[...]
    hn = _rms(x, g_ref[...]).astype(BF16)
    gate = _dot(hn, win_ref[:, :d_rnn])
    xr = _dot(hn, win_ref[:, d_rnn:])

    xbuf[halo:halo + rows, :] = xr
    xc = cb_ref[...] + cw_ref[conv_k - 1:conv_k, :] * xr
    for s in range(1, conv_k):
        xc = xc + cw_ref[conv_k - 1 - s:conv_k - s, :] * xbuf[halo - s * bsz:halo - s * bsz + rows, :]
    xbuf[0:halo, :] = xbuf[rows:rows + halo, :]

    xcb = xc.astype(BF16)
    c1 = (-0.5 * LRU_C) * _softplus(-lam_ref[...])
    for p in range(d_rnn // pair):
        cols = slice(p * pair, (p + 1) * pair)
        gp = _dot(xcb[:, cols], gw_ref[p])
        t_r = jnp.tanh(gp[:, :pair] + gb_ref[0:1, cols])
        i_t = _half_tanh_sigmoid(gp[:, pair:] + gb_ref[1:2, cols])
        log_a = c1[:, cols] * t_r + c1[:, cols]
        a = jnp.exp(log_a)
        u = jnp.tanh(log_a) * (-1.0 - a * a)
        root = u * lax.rsqrt(jnp.maximum(u, SQRT_FLOOR))
        a_s[:, cols] = a
        b_s[:, cols] = (root * xc[:, cols]) * i_t

    def step(t, h):
        r0 = pl.multiple_of(t * bsz, bsz)
        h = a_s[pl.ds(r0, bsz), :] * h + b_s[pl.ds(r0, bsz), :]
        b_s[pl.ds(r0, bsz), :] = h
        return h

    hcar[...] = lax.fori_loop(0, rows // bsz, step, hcar[...], unroll=8)

    y = (b_s[...] * _gelu_tanh(gate)).astype(BF16)
    o_ref[...] = x + _dot(y, wout_ref[...])


def _rglru(x_tm, bsz, norm_g, layer, j, w_in, conv_w, conv_b, gate_w_half, gate_b_half, lam, w_out):
    n, d = x_tm.shape
    assert bsz == V7X_SUBLANES, "one time step must fill one sublane group"
    d_rnn = w_out.shape[-2]
    conv_k = conv_w.shape[-2]
    rows = min(SEQ_ROWS, n)
    x_spec = pl.BlockSpec((rows, d), lambda t: (t, 0))
    args = [norm_g, w_in, conv_w, conv_b.reshape(-1, 1, d_rnn), gate_w_half, gate_b_half.reshape(-1, 2, d_rnn),
            lam.reshape(-1, 1, d_rnn), w_out]
    in_specs = [x_spec, _resident(norm_g.shape, (layer, 1))] + [_resident(a.shape, (j,)) for a in args[1:]]
    return pl.pallas_call(
        functools.partial(_rglru_kernel, rows=rows, bsz=bsz, d_rnn=d_rnn, conv_k=conv_k),
        grid=(n // rows,),
        in_specs=in_specs,
        out_specs=x_spec,
        out_shape=jax.ShapeDtypeStruct((n, d), F32),
        scratch_shapes=[pltpu.VMEM((rows + (conv_k - 1) * bsz, d_rnn), F32), pltpu.VMEM((rows, d_rnn), F32),
                        pltpu.VMEM((rows, d_rnn), F32), pltpu.VMEM((bsz, d_rnn), F32)],
        compiler_params=_cparams("arbitrary"),
        name="rglru",
    )(x_tm, *args)


def _pack_gate_w(gate_w):
    n, _, heads, blk, _ = gate_w.shape
    z = jnp.zeros((n, blk, blk), gate_w.dtype)
    out = []
    for p in range(heads // 2):
        h0, h1 = 2 * p, 2 * p + 1
        top = jnp.concatenate([gate_w[:, 0, h0], z, gate_w[:, 1, h0], z], axis=2)
        bot = jnp.concatenate([z, gate_w[:, 0, h1], z, gate_w[:, 1, h1]], axis=2)
        out.append(jnp.concatenate([top, bot], axis=1))
    return (0.5 * jnp.stack(out, axis=1)).astype(BF16)


def _qkv_kernel(x_ref, g_ref, w_ref, q_ref, k_ref, v_ref, *, d, scale):
    hn = _rms(x_ref[...], g_ref[...]).astype(BF16)
    q_ref[...] = (_dot(hn, w_ref[:, :d]) * scale).astype(BF16)
    k_ref[...] = _dot(hn, w_ref[:, d:2 * d]).astype(BF16)
    v_ref[...] = _dot(hn, w_ref[:, 2 * d:]).astype(BF16)


def _qkv(x, norm_g, layer, j, w_qkv):
    bsz, t_len, d = x.shape
    tm = min(FFN_ROWS, t_len)
    row_spec = pl.BlockSpec((None, tm, d), lambda b, t: (b, t, 0))
    out = jax.ShapeDtypeStruct(x.shape, BF16)
    return pl.pallas_call(
        functools.partial(_qkv_kernel, d=d, scale=SB_HEAD_DIM ** -0.5),
        grid=(bsz, t_len // tm),
        in_specs=[row_spec, _resident(norm_g.shape, (layer, 1)), _resident(w_qkv.shape, (j,))],
        out_specs=[row_spec, row_spec, row_spec],
        out_shape=[out, out, out],
        compiler_params=_cparams("arbitrary", "arbitrary"),
        name="qkv",
    )(x, norm_g, w_qkv)


def _attn_kernel(q_ref, k_ref, v_ref, x_ref, wo_ref, o_ref, qs_s, later_s, acc_s, oacc_ref, *, blk, d):
    qi = pl.program_id(1)
    lanes = V7X_LANES
    n_pairs = d // lanes
    head0 = lax.broadcasted_iota(jnp.int32, (1, lanes), 1) < SB_HEAD_DIM
    rows = lax.broadcasted_iota(jnp.int32, (2 * blk, blk), 0)
    rows = jnp.where(rows >= blk, rows - blk, rows)
    diag_mask = lax.broadcasted_iota(jnp.int32, (2 * blk, blk), 1) < rows
    jj = lax.broadcasted_iota(jnp.int32, (lanes, lanes), 0)
    kk = lax.broadcasted_iota(jnp.int32, (lanes, lanes), 1)
    tri = jnp.concatenate([(jj >= kk).astype(BF16), jnp.ones((lanes, lanes), BF16)], axis=1)
    tri = jnp.concatenate([tri, tri], axis=0)

    for p in range(n_pairs):
        qp = q_ref[:, p * lanes:(p + 1) * lanes]
        zero = jnp.zeros_like(qp)
        qs_s[p] = jnp.concatenate([jnp.where(head0, qp, zero), jnp.where(head0, zero, qp)], axis=0)

    def chain(p, kb, first):
        cs = slice(p * lanes, (p + 1) * lanes)
        k0 = pl.multiple_of(kb * blk, blk)
        kp = k_ref[pl.ds(k0, blk), cs]
        vp = v_ref[pl.ds(k0, blk), cs]
        z = lax.dot_general(qs_s[p], kp, (((1,), (1,)), ((), ())), preferred_element_type=F32)
        sp = _softplus(z)
        if first:
            sp = jnp.where(diag_mask, sp, 0.0)
        hi_f = lax.bitcast_convert_type(lax.bitcast_convert_type(sp, jnp.uint32) & jnp.uint32(0xFFFF0000), F32)
        hi = hi_f.astype(BF16)
        lo = (sp - hi_f).astype(BF16)
        sums = []
        later = None if first else later_s[p]
        for h0 in range(blk - lanes, -1, -lanes):
            out = _dot(jnp.concatenate([hi[:, h0:h0 + lanes], lo[:, h0:h0 + lanes]], axis=1), tri)
            sums.append(out[:, :lanes] if later is None else out[:, :lanes] + later)
            later = out[:, lanes:] if later is None else later + out[:, lanes:]
        w = jnp.exp(z - jnp.concatenate(sums[::-1], axis=1))
        if first:
            w = jnp.where(diag_mask, w, 0.0)
        pv = _dot(w.astype(BF16), vp)
        later_s[p] = later
        acc_s[p] = pv if first else acc_s[p] + pv

    for p in range(n_pairs):
        chain(p, qi, True)

    def older(j, carry):
        for p in range(n_pairs):
            chain(p, qi - 1 - j, False)
        return carry

    lax.fori_loop(0, qi, older, 0)

    for p in range(n_pairs):
        acc = acc_s[p]
        oacc_ref[:, p * lanes:(p + 1) * lanes] = jnp.where(head0, acc[:blk], acc[blk:]).astype(BF16)
    o_ref[...] = x_ref[...] + _dot(oacc_ref[...], wo_ref[...])


def _attn(x, q, k, v, j, w_o):
    bsz, t_len, d = x.shape
    blk = min(ATT_BLOCK, t_len)
    n_pairs = d // V7X_LANES
    q_spec = pl.BlockSpec((None, blk, d), lambda b, i: (b, i, 0))
    kv_spec = pl.BlockSpec((None, t_len, d), lambda b, i: (b, 0, 0))
    return pl.pallas_call(
        functools.partial(_attn_kernel, blk=blk, d=d),
        grid=(bsz, t_len // blk),
        in_specs=[q_spec, kv_spec, kv_spec, q_spec, _resident(w_o.shape, (j,))],
        out_specs=q_spec,
        out_shape=jax.ShapeDtypeStruct(x.shape, F32),
        scratch_shapes=[pltpu.VMEM((n_pairs, 2 * blk, V7X_LANES), BF16),
                        pltpu.VMEM((n_pairs, 2 * blk, V7X_LANES), F32),
                        pltpu.VMEM((n_pairs, 2 * blk, V7X_LANES), F32),
                        pltpu.VMEM((blk, d), BF16)],
        compiler_params=_cparams("arbitrary", "arbitrary"),
        name="sb_attn",
    )(q, k, v, x, w_o)


def _conformer_kernel(x_ref, g_ref, win_ref, bin_ref, cw_ref, cb_ref, lng_ref, lnb_ref, wout_ref, bout_ref,
                      o_ref, hbuf, conv_out, *, rows, bsz, d, conv_k):
    lanes = V7X_LANES
    n_lc = d // lanes
    halo = (conv_k - 1) * bsz

    @pl.when(pl.program_id(0) == 0)
    def _():
        hbuf[:, 0:halo, :] = jnp.zeros((n_lc, halo, lanes), F32)

    x = x_ref[...]
    hn = _rms(x, g_ref[...]).astype(BF16)
    val = _dot(hn, win_ref[:, :d]) + bin_ref[:, :d]
    gt = _dot(hn, win_ref[:, d:]) + bin_ref[:, d:]
    h = val * _half_tanh_sigmoid(0.5 * gt)
    for lc in range(n_lc):
        hbuf[lc, halo:halo + rows, :] = h[:, lc * lanes:(lc + 1) * lanes]

    for lc in range(n_lc):
        cs = slice(lc * lanes, (lc + 1) * lanes)

        def chunk(c, _, lc=lc, cs=cs):
            r0 = pl.multiple_of(c * CONV_ROWS, CONV_ROWS)
            acc = jnp.broadcast_to(cb_ref[:, cs], (CONV_ROWS, lanes))
            for s in range(conv_k):
                tap = cw_ref[conv_k - 1 - s:conv_k - s, cs]
                acc = acc + tap * hbuf[lc, pl.ds(r0 + (halo - s * bsz), CONV_ROWS), :]
            conv_out[lc, pl.ds(r0, CONV_ROWS), :] = acc
            return 0

        lax.fori_loop(0, rows // CONV_ROWS, chunk, 0)

    hbuf[:, 0:halo, :] = hbuf[:, rows:rows + halo, :]

    c = jnp.concatenate([conv_out[lc] for lc in range(n_lc)], axis=1)
    mu = jnp.mean(c, axis=-1, keepdims=True)
    cc = c - mu
    var = jnp.mean(cc * cc, axis=-1, keepdims=True)
    y = cc * lax.rsqrt(var + LN_EPS) * lng_ref[...] + lnb_ref[...]
    y = (y * _half_tanh_sigmoid(0.5 * y)).astype(BF16)
    o_ref[...] = x + _dot(y, wout_ref[...]) + bout_ref[...]


def _conformer(x_tm, bsz, norm_g, layer, j, w_in, b_in, conv_w, conv_b, ln_g, ln_b, w_out, b_out):
    n, d = x_tm.shape
    assert bsz == V7X_SUBLANES, "one time step must fill one sublane group"
    conv_k = conv_w.shape[-2]
    rows = min(SEQ_ROWS, n)
    assert (conv_k - 1) * bsz <= rows
    x_spec = pl.BlockSpec((rows, d), lambda t: (t, 0))
    args = [norm_g, w_in, b_in.reshape(-1, 1, 2 * d), conv_w, conv_b.reshape(-1, 1, d), ln_g.reshape(-1, 1, d),
            ln_b.reshape(-1, 1, d), w_out, b_out.reshape(-1, 1, d)]
    in_specs = [x_spec, _resident(norm_g.shape, (layer, 1))] + [_resident(a.shape, (j,)) for a in args[1:]]
    return pl.pallas_call(
        functools.partial(_conformer_kernel, rows=rows, bsz=bsz, d=d, conv_k=conv_k),
        grid=(n // rows,),
        in_specs=in_specs,
        out_specs=x_spec,
        out_shape=jax.ShapeDtypeStruct((n, d), F32),
        scratch_shapes=[pltpu.VMEM((d // V7X_LANES, rows + (conv_k - 1) * bsz, V7X_LANES), F32),
                        pltpu.VMEM((d // V7X_LANES, rows, V7X_LANES), F32)],
        compiler_params=_cparams("arbitrary"),
        name="conformer",
    )(x_tm, *args)


def kernel(x, norm_g, ffn_w_in, ffn_w_out, a_w_in, a_conv_w, a_conv_b, a_gate_w, a_gate_b, a_lambda, a_w_out,
           b_w_qkv, b_w_o, c_w_in, c_b_in, c_conv_w, c_conv_b, c_ln_g, c_ln_b, c_w_out, c_b_out, final_g):
    bsz, t_len, d = x.shape
    depth = norm_g.shape[0]
    n_mixers = 3
    attention = 1
    norm_g = norm_g.reshape(depth, 3, 1, d)
    w_in, w_out = ffn_w_in[0, 0].astype(BF16), ffn_w_out[0, 0].astype(BF16)
    a_w_in, a_w_out = a_w_in.astype(BF16), a_w_out.astype(BF16)
    a_gate_w, a_gate_b = _pack_gate_w(a_gate_w), 0.5 * a_gate_b
    b_w_qkv, b_w_o = b_w_qkv.astype(BF16), b_w_o.astype(BF16)
    c_w_in, c_w_out = c_w_in.astype(BF16), c_w_out.astype(BF16)

    time_major = False
    for layer in range(depth):
        kind, j = layer % n_mixers, layer // n_mixers
        mixer_tm = kind != attention
        x, w_in, w_out = _ffn(x, norm_g, w_in, w_out, layer, 0, bsz, time_major, mixer_tm,
                              next_w=(ffn_w_in, ffn_w_out, layer, 1))
        if kind == 0:
            x = _rglru(x, bsz, norm_g, layer, j, a_w_in, a_conv_w, a_conv_b, a_gate_w, a_gate_b, a_lambda, a_w_out)
        elif kind == attention:
            q, k, v = _qkv(x, norm_g, layer, j, b_w_qkv)
            x = _attn(x, q, k, v, j, b_w_o)
        else:
            x = _conformer(x, bsz, norm_g, layer, j, c_w_in, c_b_in, c_conv_w, c_conv_b, c_ln_g, c_ln_b, c_w_out,
                           c_b_out)
        last = layer == depth - 1
        time_major = (not last) and (layer + 1) % n_mixers != attention
        if last:
            (x,) = _ffn(x, norm_g, w_in, w_out, layer, 1, bsz, mixer_tm, time_major, final_g)
        else:
            x, w_in, w_out = _ffn(x, norm_g, w_in, w_out, layer, 1, bsz, mixer_tm, time_major,
                                  next_w=(ffn_w_in, ffn_w_out, layer + 1, 0))
    return x
```

```python
import functools

import jax
import jax.numpy as jnp
from jax import lax
from jax.experimental import pallas as pl
from jax.experimental.pallas import tpu as pltpu

F32 = jnp.float32
BF16 = jnp.bfloat16

RMS_EPS = 1e-6
LN_EPS = 1e-5
LRU_C = 8.0
SB_HEAD_DIM = 64
LOG2_E = 1.4426950408889634

V7X_LANES = 128
V7X_SUBLANES = 8
V7X_VMEM_LIMIT_BYTES = 56 * 1024 * 1024

FFN_ROWS = 1024
FFN_CHUNK = 512
SEQ_ROWS = 512
CONV_ROWS = 64
SQRT_FLOOR = 1e-37
ATT_BLOCK = 256


def _cparams(*sem):
    return pltpu.CompilerParams(dimension_semantics=sem, vmem_limit_bytes=V7X_VMEM_LIMIT_BYTES)


def _resident(shape, lead=()):
    block = (None,) * len(lead) + tuple(shape[len(lead):])
    idx = tuple(lead) + (0,) * (len(shape) - len(lead))
    return pl.BlockSpec(block, lambda *_: idx, pipeline_mode=pl.Buffered(1))


def _rms(x, g):
    ms = jnp.mean(x * x, axis=-1, keepdims=True)
    return x * lax.rsqrt(ms + RMS_EPS) * g


def _dot(a, b):
    return jnp.dot(a, b, preferred_element_type=F32)


def _softplus(z):
    return jnp.maximum(z, 0.0) + jnp.log(1.0 + jnp.exp2(jnp.abs(z) * (-LOG2_E)))


def _half_tanh_sigmoid(half_x):
    return 0.5 * jnp.tanh(half_x) + 0.5


def _gelu_tanh(x):
    c0 = 0.7978845608028654
    inner = x * (c0 + (c0 * 0.044715) * (x * x))
    return (0.5 * x) * (1.0 + jnp.tanh(inner))


def _row_spec(time_major, swap, tm, bsz, t_len, d):
    if time_major:
        return pl.BlockSpec((tm, d), lambda i: (i, 0))
    if swap:
        return pl.BlockSpec((bsz, tm // bsz, d), lambda i: (0, i, 0))
    per_batch = t_len // tm
    return pl.BlockSpec((None, tm, d), lambda i: (i // per_batch, i % per_batch, 0))


def _ffn_kernel(*refs, d_ff, final, in_swap, out_swap, n_cast):
    refs = list(refs)
    x_ref, g_ref, win_ref, wout_ref = refs[:4]
    del refs[:4]
    fg_ref = refs.pop(0) if final else None
    cast_in, o_ref, cast_out, act_ref = refs[:n_cast], refs[n_cast], refs[n_cast + 1:-1], refs[-1]
    for src, dst in zip(cast_in, cast_out):
        dst[...] = src[...].astype(BF16)
    x = x_ref[...]
    if in_swap:
        x = jnp.swapaxes(x, 0, 1).reshape(-1, x.shape[-1])
    xn = _rms(x, g_ref[...]).astype(BF16)
    for c0 in range(0, d_ff, FFN_CHUNK):
        cw = min(FFN_CHUNK, d_ff - c0)
        gate = _dot(xn, win_ref[:, c0:c0 + cw])
        up = _dot(xn, win_ref[:, d_ff + c0:d_ff + c0 + cw])
        act_ref[:, c0:c0 + cw] = (gate * jax.nn.sigmoid(gate) * up).astype(BF16)
    out = x + 0.5 * _dot(act_ref[...], wout_ref[...])
    if final:
        out = _rms(out, fg_ref[...])
    if out_swap:
        bsz = o_ref.shape[0]
        out = jnp.swapaxes(out.reshape(-1, bsz, out.shape[-1]), 0, 1)
    o_ref[...] = out


def _ffn(x, norm_g, w_in, w_out, layer, which, bsz, in_tm, out_tm, final_g=None, casts=()):
    d = x.shape[-1]
    t_len = x.shape[0] // bsz if in_tm else x.shape[1]
    d_ff = w_out.shape[-2]
    tm = min(FFN_ROWS, t_len)
    steps = bsz * t_len // tm
    final = final_g is not None
    in_swap, out_swap = (not in_tm) and out_tm, in_tm and not out_tm
    in_specs = [_row_spec(in_tm, in_swap, tm, bsz, t_len, d), _resident(norm_g.shape, (layer, 2 * which)),
                _resident(w_in.shape), _resident(w_out.shape)]
    args = [x, norm_g, w_in, w_out]
    out_specs = [_row_spec(out_tm, out_swap, tm, bsz, t_len, d)]
    out_shape = [jax.ShapeDtypeStruct((t_len * bsz, d) if out_tm else (bsz, t_len, d), F32)]
    if final:
        in_specs.append(_resident((1, d)))
        args.append(final_g.reshape(1, d))
    for w_all, lead in casts:
        r, c = w_all.shape[-2:]
        assert r % (steps * 2 * V7X_SUBLANES) == 0 and len(lead) == w_all.ndim - 2
        in_specs.append(pl.BlockSpec((None,) * len(lead) + (r // steps, c), lambda i, lead=lead: (*lead, i, 0)))
        args.append(w_all)
        out_specs.append(pl.BlockSpec((r // steps, c), lambda i: (i, 0)))
        out_shape.append(jax.ShapeDtypeStruct((r, c), BF16))
    return pl.pallas_call(
        functools.partial(_ffn_kernel, d_ff=d_ff, final=final, in_swap=in_swap, out_swap=out_swap,
                          n_cast=len(casts)),
        grid=(steps,),
        in_specs=in_specs,
        out_specs=out_specs,
        out_shape=out_shape,
        scratch_shapes=[pltpu.VMEM((tm, d_ff), BF16)],
        compiler_params=_cparams("arbitrary"),
        name="ffn_final" if final else "ffn",
    )(*args)


def _rglru_kernel(x_ref, g_ref, win_ref, cw_ref, cb_ref, gw_ref, gb_ref, lam_ref, wout_ref, o_ref,
                  xbuf, a_s, b_s, hcar, *, rows, bsz, d_rnn, conv_k):
    pair = 2 * V7X_LANES
    halo = (conv_k - 1) * bsz

    @pl.when(pl.program_id(0) == 0)
    def _():
        xbuf[0:halo, :] = jnp.zeros((halo, d_rnn), F32)
        hcar[...] = jnp.zeros_like(hcar)

    x = x_ref[...]
    hn = _rms(x, g_ref[...]).astype(BF16)
    gate = _dot(hn, win_ref[:, :d_rnn])
    xr = _dot(hn, win_ref[:, d_rnn:])

    xbuf[halo:halo + rows, :] = xr
    xc = cb_ref[...] + cw_ref[conv_k - 1:conv_k, :] * xr
    for s in range(1, conv_k):
        xc = xc + cw_ref[conv_k - 1 - s:conv_k - s, :] * xbuf[halo - s * bsz:halo - s * bsz + rows, :]
    xbuf[0:halo, :] = xbuf[rows:rows + halo, :]

    xcb = xc.astype(BF16)
    c1 = (-0.5 * LRU_C) * _softplus(-lam_ref[...])
    for p in range(d_rnn // pair):
        cols = slice(p * pair, (p + 1) * pair)
        gp = _dot(xcb[:, cols], gw_ref[p])
        t_r = jnp.tanh(gp[:, :pair] + gb_ref[0:1, cols])
        i_t = _half_tanh_sigmoid(gp[:, pair:] + gb_ref[1:2, cols])
        log_a = c1[:, cols] * t_r + c1[:, cols]
        a = jnp.exp(log_a)
        u = jnp.tanh(log_a) * (-1.0 - a * a)
        root = u * lax.rsqrt(jnp.maximum(u, SQRT_FLOOR))
        a_s[:, cols] = a
        b_s[:, cols] = (root * xc[:, cols]) * i_t

    h = hcar[...]
    for t in range(rows // bsz):
        h = a_s[t * bsz:(t + 1) * bsz, :] * h + b_s[t * bsz:(t + 1) * bsz, :]
        b_s[t * bsz:(t + 1) * bsz, :] = h
    hcar[...] = h

    y = (b_s[...] * _gelu_tanh(gate)).astype(BF16)
    o_ref[...] = x + _dot(y, wout_ref[...])


def _rglru(x_tm, bsz, norm_g, layer, j, w_in, conv_w, conv_b, gate_w_half, gate_b_half, lam, w_out):
    n, d = x_tm.shape
    assert bsz == V7X_SUBLANES, "one time step must fill one sublane group"
    d_rnn = w_out.shape[-2]
    conv_k = conv_w.shape[-2]
    rows = min(SEQ_ROWS, n)
    x_spec = pl.BlockSpec((rows, d), lambda t: (t, 0))
    args = [norm_g, w_in, conv_w, conv_b.reshape(-1, 1, d_rnn), gate_w_half, gate_b_half.reshape(-1, 2, d_rnn),
            lam.reshape(-1, 1, d_rnn), w_out]
    in_specs = [x_spec, _resident(norm_g.shape, (layer, 1))]
    in_specs += [_resident(a.shape, () if a.ndim == 2 else (j,)) for a in args[1:]]
    return pl.pallas_call(
        functools.partial(_rglru_kernel, rows=rows, bsz=bsz, d_rnn=d_rnn, conv_k=conv_k),
        grid=(n // rows,),
        in_specs=in_specs,
        out_specs=x_spec,
        out_shape=jax.ShapeDtypeStruct((n, d), F32),
        scratch_shapes=[pltpu.VMEM((rows + (conv_k - 1) * bsz, d_rnn), F32), pltpu.VMEM((rows, d_rnn), F32),
                        pltpu.VMEM((rows, d_rnn), F32), pltpu.VMEM((bsz, d_rnn), F32)],
        compiler_params=_cparams("arbitrary"),
        name="rglru",
    )(x_tm, *args)


def _pack_gate_w(gate_w):
    n, _, heads, blk, _ = gate_w.shape
    z = jnp.zeros((n, blk, blk), gate_w.dtype)
    out = []
    for p in range(heads // 2):
        h0, h1 = 2 * p, 2 * p + 1
        top = jnp.concatenate([gate_w[:, 0, h0], z, gate_w[:, 1, h0], z], axis=2)
        bot = jnp.concatenate([z, gate_w[:, 0, h1], z, gate_w[:, 1, h1]], axis=2)
        out.append(jnp.concatenate([top, bot], axis=1))
    return (0.5 * jnp.stack(out, axis=1)).astype(BF16)


def _qkv_kernel(x_ref, g_ref, w_ref, q_ref, k_ref, v_ref, *, d, scale):
    hn = _rms(x_ref[...], g_ref[...]).astype(BF16)
    q_ref[...] = (_dot(hn, w_ref[:, :d]) * scale).astype(BF16)
    k_ref[...] = _dot(hn, w_ref[:, d:2 * d]).astype(BF16)
    v_ref[...] = _dot(hn, w_ref[:, 2 * d:]).astype(BF16)


def _qkv(x, norm_g, layer, w_qkv):
    bsz, t_len, d = x.shape
    tm = min(FFN_ROWS, t_len)
    row_spec = pl.BlockSpec((None, tm, d), lambda b, t: (b, t, 0))
    out = jax.ShapeDtypeStruct(x.shape, BF16)
    return pl.pallas_call(
        functools.partial(_qkv_kernel, d=d, scale=SB_HEAD_DIM ** -0.5),
        grid=(bsz, t_len // tm),
        in_specs=[row_spec, _resident(norm_g.shape, (layer, 1)), _resident(w_qkv.shape)],
        out_specs=[row_spec, row_spec, row_spec],
        out_shape=[out, out, out],
        compiler_params=_cparams("arbitrary", "arbitrary"),
        name="qkv",
    )(x, norm_g, w_qkv)


def _attn_kernel(q_ref, k_ref, v_ref, x_ref, wo_ref, o_ref, qs_s, later_s, acc_s, oacc_ref, *, blk, d):
    qi = pl.program_id(1)
    lanes = V7X_LANES
    n_pairs = d // lanes
    head0 = lax.broadcasted_iota(jnp.int32, (1, lanes), 1) < SB_HEAD_DIM
    rows = lax.broadcasted_iota(jnp.int32, (2 * blk, blk), 0)
    rows = jnp.where(rows >= blk, rows - blk, rows)
    diag_mask = lax.broadcasted_iota(jnp.int32, (2 * blk, blk), 1) < rows
    jj = lax.broadcasted_iota(jnp.int32, (lanes, lanes), 0)
    kk = lax.broadcasted_iota(jnp.int32, (lanes, lanes), 1)
    tri = jnp.concatenate([(jj >= kk).astype(BF16), jnp.ones((lanes, lanes), BF16)], axis=1)
    tri = jnp.concatenate([tri, tri], axis=0)

    for p in range(n_pairs):
        qp = q_ref[:, p * lanes:(p + 1) * lanes]
        zero = jnp.zeros_like(qp)
        qs_s[p] = jnp.concatenate([jnp.where(head0, qp, zero), jnp.where(head0, zero, qp)], axis=0)

    def chain(p, kb, first):
        cs = slice(p * lanes, (p + 1) * lanes)
        k0 = pl.multiple_of(kb * blk, blk)
        kp = k_ref[pl.ds(k0, blk), cs]
        vp = v_ref[pl.ds(k0, blk), cs]
        z = lax.dot_general(qs_s[p], kp, (((1,), (1,)), ((), ())), preferred_element_type=F32)
        sp = _softplus(z)
        if first:
            sp = jnp.where(diag_mask, sp, 0.0)
        hi_f = lax.bitcast_convert_type(lax.bitcast_convert_type(sp, jnp.uint32) & jnp.uint32(0xFFFF0000), F32)
        hi = hi_f.astype(BF16)
        lo = (sp - hi_f).astype(BF16)
        sums = []
        later = None if first else later_s[p]
        for h0 in range(blk - lanes, -1, -lanes):
            out = _dot(jnp.concatenate([hi[:, h0:h0 + lanes], lo[:, h0:h0 + lanes]], axis=1), tri)
            sums.append(out[:, :lanes] if later is None else out[:, :lanes] + later)
            later = out[:, lanes:] if later is None else later + out[:, lanes:]
        w = jnp.exp(z - jnp.concatenate(sums[::-1], axis=1))
        if first:
            w = jnp.where(diag_mask, w, 0.0)
        pv = _dot(w.astype(BF16), vp)
        later_s[p] = later
        acc_s[p] = pv if first else acc_s[p] + pv

    for p in range(n_pairs):
        chain(p, qi, True)

    def older(j, carry):
        for p in range(n_pairs):
            chain(p, qi - 1 - j, False)
        return carry

    lax.fori_loop(0, qi, older, 0)

    for p in range(n_pairs):
        acc = acc_s[p]
        oacc_ref[:, p * lanes:(p + 1) * lanes] = jnp.where(head0, acc[:blk], acc[blk:]).astype(BF16)
    o_ref[...] = x_ref[...] + _dot(oacc_ref[...], wo_ref[...])


def _attn(x, q, k, v, w_o):
    bsz, t_len, d = x.shape
    blk = min(ATT_BLOCK, t_len)
    n_pairs = d // V7X_LANES
    q_spec = pl.BlockSpec((None, blk, d), lambda b, i: (b, i, 0))
    kv_spec = pl.BlockSpec((None, t_len, d), lambda b, i: (b, 0, 0))
    return pl.pallas_call(
        functools.partial(_attn_kernel, blk=blk, d=d),
        grid=(bsz, t_len // blk),
        in_specs=[q_spec, kv_spec, kv_spec, q_spec, _resident(w_o.shape)],
        out_specs=q_spec,
        out_shape=jax.ShapeDtypeStruct(x.shape, F32),
        scratch_shapes=[pltpu.VMEM((n_pairs, 2 * blk, V7X_LANES), BF16),
                        pltpu.VMEM((n_pairs, 2 * blk, V7X_LANES), F32),
                        pltpu.VMEM((n_pairs, 2 * blk, V7X_LANES), F32),
                        pltpu.VMEM((blk, d), BF16)],
        compiler_params=_cparams("arbitrary", "arbitrary"),
        name="sb_attn",
    )(q, k, v, x, w_o)


def _conformer_kernel(x_ref, g_ref, win_ref, bin_ref, cw_ref, cb_ref, lng_ref, lnb_ref, wout_ref, bout_ref,
                      o_ref, hbuf, conv_out, *, rows, bsz, d, conv_k):
    lanes = V7X_LANES
    n_lc = d // lanes
    halo = (conv_k - 1) * bsz

    @pl.when(pl.program_id(0) == 0)
    def _():
        hbuf[:, 0:halo, :] = jnp.zeros((n_lc, halo, lanes), F32)

    x = x_ref[...]
    hn = _rms(x, g_ref[...]).astype(BF16)
    val = _dot(hn, win_ref[:, :d]) + bin_ref[:, :d]
    gt = _dot(hn, win_ref[:, d:]) + bin_ref[:, d:]
    h = val * _half_tanh_sigmoid(0.5 * gt)
    for lc in range(n_lc):
        hbuf[lc, halo:halo + rows, :] = h[:, lc * lanes:(lc + 1) * lanes]

    for lc in range(n_lc):
        cs = slice(lc * lanes, (lc + 1) * lanes)

        def chunk(c, _, lc=lc, cs=cs):
            r0 = pl.multiple_of(c * CONV_ROWS, CONV_ROWS)
            acc = jnp.broadcast_to(cb_ref[:, cs], (CONV_ROWS, lanes))
            for s in range(conv_k):
                tap = cw_ref[conv_k - 1 - s:conv_k - s, cs]
                acc = acc + tap * hbuf[lc, pl.ds(r0 + (halo - s * bsz), CONV_ROWS), :]
            conv_out[lc, pl.ds(r0, CONV_ROWS), :] = acc
            return 0

        lax.fori_loop(0, rows // CONV_ROWS, chunk, 0)

    hbuf[:, 0:halo, :] = hbuf[:, rows:rows + halo, :]

    c = jnp.concatenate([conv_out[lc] for lc in range(n_lc)], axis=1)
    mu = jnp.mean(c, axis=-1, keepdims=True)
    cc = c - mu
    var = jnp.mean(cc * cc, axis=-1, keepdims=True)
    y = cc * lax.rsqrt(var + LN_EPS) * lng_ref[...] + lnb_ref[...]
    y = (y * _half_tanh_sigmoid(0.5 * y)).astype(BF16)
    o_ref[...] = x + _dot(y, wout_ref[...]) + bout_ref[...]


def _conformer(x_tm, bsz, norm_g, layer, j, w_in, b_in, conv_w, conv_b, ln_g, ln_b, w_out, b_out):
    n, d = x_tm.shape
    assert bsz == V7X_SUBLANES, "one time step must fill one sublane group"
    conv_k = conv_w.shape[-2]
    rows = min(SEQ_ROWS, n)
    assert (conv_k - 1) * bsz <= rows
    x_spec = pl.BlockSpec((rows, d), lambda t: (t, 0))
    args = [norm_g, w_in, b_in.reshape(-1, 1, 2 * d), conv_w, conv_b.reshape(-1, 1, d), ln_g.reshape(-1, 1, d),
            ln_b.reshape(-1, 1, d), w_out, b_out.reshape(-1, 1, d)]
    in_specs = [x_spec, _resident(norm_g.shape, (layer, 1))]
    in_specs += [_resident(a.shape, () if a.ndim == 2 else (j,)) for a in args[1:]]
    return pl.pallas_call(
        functools.partial(_conformer_kernel, rows=rows, bsz=bsz, d=d, conv_k=conv_k),
        grid=(n // rows,),
        in_specs=in_specs,
        out_specs=x_spec,
        out_shape=jax.ShapeDtypeStruct((n, d), F32),
        scratch_shapes=[pltpu.VMEM((d // V7X_LANES, rows + (conv_k - 1) * bsz, V7X_LANES), F32),
                        pltpu.VMEM((d // V7X_LANES, rows, V7X_LANES), F32)],
        compiler_params=_cparams("arbitrary"),
        name="conformer",
    )(x_tm, *args)


def kernel(x, norm_g, ffn_w_in, ffn_w_out, a_w_in, a_conv_w, a_conv_b, a_gate_w, a_gate_b, a_lambda, a_w_out,
           b_w_qkv, b_w_o, c_w_in, c_b_in, c_conv_w, c_conv_b, c_ln_g, c_ln_b, c_w_out, c_b_out, final_g):
    bsz, t_len, d = x.shape
    depth = norm_g.shape[0]
    n_mixers = 3
    attention = 1
    norm_g = norm_g.reshape(depth, 3, 1, d)
    w_in, w_out = ffn_w_in[0, 0].astype(BF16), ffn_w_out[0, 0].astype(BF16)
    a_gate_w, a_gate_b = _pack_gate_w(a_gate_w), 0.5 * a_gate_b
    mixer_w = ((a_w_in, a_w_out), (b_w_qkv, b_w_o), (c_w_in, c_w_out))

    time_major = False
    for layer in range(depth):
        kind, j = layer % n_mixers, layer // n_mixers
        mixer_tm = kind != attention
        casts = [(ffn_w_in, (layer, 1)), (ffn_w_out, (layer, 1))] + [(w, (j,)) for w in mixer_w[kind]]
        x, w_in, w_out, m_in, m_out = _ffn(x, norm_g, w_in, w_out, layer, 0, bsz, time_major, mixer_tm, casts=casts)
        if kind == 0:
            x = _rglru(x, bsz, norm_g, layer, j, m_in, a_conv_w, a_conv_b, a_gate_w, a_gate_b, a_lambda, m_out)
        elif kind == attention:
            q, k, v = _qkv(x, norm_g, layer, m_in)
            x = _attn(x, q, k, v, m_out)
        else:
            x = _conformer(x, bsz, norm_g, layer, j, m_in, c_b_in, c_conv_w, c_conv_b, c_ln_g, c_ln_b, m_out,
                           c_b_out)
        last = layer == depth - 1
        time_major = (not last) and (layer + 1) % n_mixers != attention
        if last:
            (x,) = _ffn(x, norm_g, w_in, w_out, layer, 1, bsz, mixer_tm, time_major, final_g)
        else:
            casts = [(ffn_w_in, (layer + 1, 0)), (ffn_w_out, (layer + 1, 0))]
            x, w_in, w_out = _ffn(x, norm_g, w_in, w_out, layer, 1, bsz, mixer_tm, time_major, casts=casts)
    return x
```

```python
import functools

import jax
import jax.numpy as jnp
from jax import lax
from jax.experimental import pallas as pl
from jax.experimental.pallas import tpu as pltpu

F32 = jnp.float32
BF16 = jnp.bfloat16

RMS_EPS = 1e-6
LN_EPS = 1e-5
LRU_C = 8.0
SB_HEAD_DIM = 64
LOG2_E = 1.4426950408889634

V7X_LANES = 128
V7X_SUBLANES = 8
V7X_VMEM_LIMIT_BYTES = 56 * 1024 * 1024

FFN_ROWS = 1024
FFN_CHUNK = 512
SEQ_ROWS = 1024
CONV_ROWS = 128
SQRT_FLOOR = 1e-37
ATT_BLOCK = 256
BF16_BITS_OF_F32 = 0xFFFF0000


def _cparams(*sem):
    return pltpu.CompilerParams(dimension_semantics=sem, vmem_limit_bytes=V7X_VMEM_LIMIT_BYTES)


def _resident(shape, lead=()):
    block = (None,) * len(lead) + tuple(shape[len(lead):])
    idx = tuple(lead) + (0,) * (len(shape) - len(lead))
    return pl.BlockSpec(block, lambda *_: idx, pipeline_mode=pl.Buffered(1))


def _rms(x, g):
    ms = jnp.mean(x * x, axis=-1, keepdims=True)
    return x * lax.rsqrt(ms + RMS_EPS) * g


def _dot(a, b):
    return jnp.dot(a, b, preferred_element_type=F32)


def _softplus(z):
    return jnp.maximum(z, 0.0) + jnp.log(1.0 + jnp.exp2(jnp.abs(z) * (-LOG2_E)))


def _half_tanh_sigmoid(half_x):
    return 0.5 * jnp.tanh(half_x) + 0.5


def _gelu_tanh(x):
    c0 = 0.7978845608028654
    inner = x * (c0 + (c0 * 0.044715) * (x * x))
    return (0.5 * x) * (1.0 + jnp.tanh(inner))


def _row_spec(time_major, swap, tm, bsz, t_len, d):
    if time_major:
        return pl.BlockSpec((tm, d), lambda i: (i, 0))
    if swap:
        return pl.BlockSpec((bsz, tm // bsz, d), lambda i: (0, i, 0))
    per_batch = t_len // tm
    return pl.BlockSpec((None, tm, d), lambda i: (i // per_batch, i % per_batch, 0))


def _ffn_kernel(*refs, d_ff, final, in_swap, out_swap, n_cast):
    refs = list(refs)
    x_ref, g_ref, win_ref, wout_ref = refs[:4]
    del refs[:4]
    fg_ref = refs.pop(0) if final else None
    cast_in, o_ref, cast_out, act_ref = refs[:n_cast], refs[n_cast], refs[n_cast + 1:-1], refs[-1]
    for src, dst in zip(cast_in, cast_out):
        dst[...] = src[...].astype(BF16)
    x = x_ref[...]
    if in_swap:
        x = jnp.swapaxes(x, 0, 1).reshape(-1, x.shape[-1])
    xn = _rms(x, g_ref[...]).astype(BF16)
    for c0 in range(0, d_ff, FFN_CHUNK):
        cw = min(FFN_CHUNK, d_ff - c0)
        gate = _dot(xn, win_ref[:, c0:c0 + cw])
        up = _dot(xn, win_ref[:, d_ff + c0:d_ff + c0 + cw])
        act_ref[:, c0:c0 + cw] = (gate * jax.nn.sigmoid(gate) * up).astype(BF16)
    out = x + 0.5 * _dot(act_ref[...], wout_ref[...])
    if final:
        out = _rms(out, fg_ref[...])
    if out_swap:
        bsz = o_ref.shape[0]
        out = jnp.swapaxes(out.reshape(-1, bsz, out.shape[-1]), 0, 1)
    o_ref[...] = out


def _ffn(x, norm_g, w_in, w_out, layer, which, bsz, in_tm, out_tm, final_g=None, casts=()):
    d = x.shape[-1]
    t_len = x.shape[0] // bsz if in_tm else x.shape[1]
    d_ff = w_out.shape[-2]
    tm = min(FFN_ROWS, t_len)
    steps = bsz * t_len // tm
    final = final_g is not None
    in_swap, out_swap = (not in_tm) and out_tm, in_tm and not out_tm
    in_specs = [_row_spec(in_tm, in_swap, tm, bsz, t_len, d), _resident(norm_g.shape, (layer, 2 * which)),
                _resident(w_in.shape), _resident(w_out.shape)]
    args = [x, norm_g, w_in, w_out]
    out_specs = [_row_spec(out_tm, out_swap, tm, bsz, t_len, d)]
    out_shape = [jax.ShapeDtypeStruct((t_len * bsz, d) if out_tm else (bsz, t_len, d), F32)]
    if final:
        in_specs.append(_resident((1, d)))
        args.append(final_g.reshape(1, d))
    for w_all, lead in casts:
        r, c = w_all.shape[-2:]
        assert r % (steps * 2 * V7X_SUBLANES) == 0 and len(lead) == w_all.ndim - 2
        in_specs.append(pl.BlockSpec((None,) * len(lead) + (r // steps, c), lambda i, lead=lead: (*lead, i, 0)))
        args.append(w_all)
        out_specs.append(pl.BlockSpec((r // steps, c), lambda i: (i, 0)))
        out_shape.append(jax.ShapeDtypeStruct((r, c), BF16))
    return pl.pallas_call(
        functools.partial(_ffn_kernel, d_ff=d_ff, final=final, in_swap=in_swap, out_swap=out_swap,
                          n_cast=len(casts)),
        grid=(steps,),
        in_specs=in_specs,
        out_specs=out_specs,
        out_shape=out_shape,
        scratch_shapes=[pltpu.VMEM((tm, d_ff), BF16)],
        compiler_params=_cparams("arbitrary"),
        name="ffn_final" if final else "ffn",
    )(*args)


def _rglru_kernel(x_ref, g_ref, win_ref, cw_ref, cb_ref, gw_ref, gb_ref, lam_ref, wout_ref, o_ref,
                  xbuf, a_s, b_s, hcar, *, rows, bsz, d_rnn, conv_k):
    pair = 2 * V7X_LANES
    halo = (conv_k - 1) * bsz

    @pl.when(pl.program_id(0) == 0)
    def _():
        xbuf[0:halo, :] = jnp.zeros((halo, d_rnn), F32)
        hcar[...] = jnp.zeros_like(hcar)

    x = x_ref[...]
    hn = _rms(x, g_ref[...]).astype(BF16)
    gate = _dot(hn, win_ref[:, :d_rnn])
    xr = _dot(hn, win_ref[:, d_rnn:])

    xbuf[halo:halo + rows, :] = xr
    xc = cb_ref[...] + cw_ref[conv_k - 1:conv_k, :] * xr
    for s in range(1, conv_k):
        xc = xc + cw_ref[conv_k - 1 - s:conv_k - s, :] * xbuf[halo - s * bsz:halo - s * bsz + rows, :]
    xbuf[0:halo, :] = xbuf[rows:rows + halo, :]

    xcb = xc.astype(BF16)
    c1 = (-0.5 * LRU_C) * _softplus(-lam_ref[...])
    for p in range(d_rnn // pair):
        cols = slice(p * pair, (p + 1) * pair)
        gp = _dot(xcb[:, cols], gw_ref[p])
        t_r = jnp.tanh(gp[:, :pair] + gb_ref[0:1, cols])
        i_t = _half_tanh_sigmoid(gp[:, pair:] + gb_ref[1:2, cols])
        log_a = c1[:, cols] * t_r + c1[:, cols]
        a = jnp.exp(log_a)
        u = jnp.tanh(log_a) * (-1.0 - a * a)
        root = u * lax.rsqrt(jnp.maximum(u, SQRT_FLOOR))
        a_s[:, cols] = a
        b_s[:, cols] = (root * xc[:, cols]) * i_t

    h = hcar[...]
    for t in range(rows // bsz):
        h = a_s[t * bsz:(t + 1) * bsz, :] * h + b_s[t * bsz:(t + 1) * bsz, :]
        b_s[t * bsz:(t + 1) * bsz, :] = h
    hcar[...] = h

    y = (b_s[...] * _gelu_tanh(gate)).astype(BF16)
    o_ref[...] = x + _dot(y, wout_ref[...])


def _rglru(x_tm, bsz, norm_g, layer, j, w_in, conv_w, conv_b, gate_w_half, gate_b_half, lam, w_out):
    n, d = x_tm.shape
    assert bsz == V7X_SUBLANES, "one time step must fill one sublane group"
    d_rnn = w_out.shape[-2]
    conv_k = conv_w.shape[-2]
    rows = min(SEQ_ROWS, n)
    x_spec = pl.BlockSpec((rows, d), lambda t: (t, 0))
    args = [norm_g, w_in, conv_w, conv_b.reshape(-1, 1, d_rnn), gate_w_half, gate_b_half.reshape(-1, 2, d_rnn),
            lam.reshape(-1, 1, d_rnn), w_out]
    in_specs = [x_spec, _resident(norm_g.shape, (layer, 1))]
    in_specs += [_resident(a.shape, () if a.ndim == 2 else (j,)) for a in args[1:]]
    return pl.pallas_call(
        functools.partial(_rglru_kernel, rows=rows, bsz=bsz, d_rnn=d_rnn, conv_k=conv_k),
        grid=(n // rows,),
        in_specs=in_specs,
        out_specs=x_spec,
        out_shape=jax.ShapeDtypeStruct((n, d), F32),
        scratch_shapes=[pltpu.VMEM((rows + (conv_k - 1) * bsz, d_rnn), F32), pltpu.VMEM((rows, d_rnn), F32),
                        pltpu.VMEM((rows, d_rnn), F32), pltpu.VMEM((bsz, d_rnn), F32)],
        compiler_params=_cparams("arbitrary"),
        name="rglru",
    )(x_tm, *args)


def _pack_gate_w(gate_w):
    n, _, heads, blk, _ = gate_w.shape
    z = jnp.zeros((n, blk, blk), gate_w.dtype)
    out = []
    for p in range(heads // 2):
        h0, h1 = 2 * p, 2 * p + 1
        top = jnp.concatenate([gate_w[:, 0, h0], z, gate_w[:, 1, h0], z], axis=2)
        bot = jnp.concatenate([z, gate_w[:, 0, h1], z, gate_w[:, 1, h1]], axis=2)
        out.append(jnp.concatenate([top, bot], axis=1))
    return (0.5 * jnp.stack(out, axis=1)).astype(BF16)


def _attn_kernel(x_ref, g_ref, wqkv_ref, wo_ref, o_ref, k_ref, v_ref, qs_s, later_s, acc_s, oacc_ref, *, blk, d):
    qi = pl.program_id(1)
    lanes = V7X_LANES
    n_pairs = d // lanes
    head0 = lax.broadcasted_iota(jnp.int32, (1, lanes), 1) < SB_HEAD_DIM
    rows = lax.broadcasted_iota(jnp.int32, (2 * blk, blk), 0)
    rows = jnp.where(rows >= blk, rows - blk, rows)
    diag_mask = lax.broadcasted_iota(jnp.int32, (2 * blk, blk), 1) < rows
    jj = lax.broadcasted_iota(jnp.int32, (lanes, lanes), 0)
    kk = lax.broadcasted_iota(jnp.int32, (lanes, lanes), 1)
    tri = jnp.concatenate([(jj >= kk).astype(BF16), jnp.ones((lanes, lanes), BF16)], axis=1)
    tri = jnp.concatenate([tri, tri], axis=0)

    x = x_ref[...]
    hn = _rms(x, g_ref[...]).astype(BF16)
    q = (_dot(hn, wqkv_ref[:, :d]) * (SB_HEAD_DIM ** -0.5)).astype(BF16)
    own = pl.ds(pl.multiple_of(qi * blk, blk), blk)
    k_ref[own, :] = _dot(hn, wqkv_ref[:, d:2 * d]).astype(BF16)
    v_ref[own, :] = _dot(hn, wqkv_ref[:, 2 * d:]).astype(BF16)
    for p in range(n_pairs):
        qp = q[:, p * lanes:(p + 1) * lanes]
        zero = jnp.zeros_like(qp)
        qs_s[p] = jnp.concatenate([jnp.where(head0, qp, zero), jnp.where(head0, zero, qp)], axis=0)

    def chain(p, kb, first):
        cs = slice(p * lanes, (p + 1) * lanes)
        k0 = pl.multiple_of(kb * blk, blk)
        kp = k_ref[pl.ds(k0, blk), cs]
        vp = v_ref[pl.ds(k0, blk), cs]
        z = lax.dot_general(qs_s[p], kp, (((1,), (1,)), ((), ())), preferred_element_type=F32)
        sp = _softplus(z)
        if first:
            sp = jnp.where(diag_mask, sp, 0.0)
        hi_f = lax.bitcast_convert_type(lax.bitcast_convert_type(sp, jnp.uint32) & jnp.uint32(BF16_BITS_OF_F32), F32)
        hi = hi_f.astype(BF16)
        lo = (sp - hi_f).astype(BF16)
        sums = []
        later = None if first else later_s[p]
        for h0 in range(blk - lanes, -1, -lanes):
            out = _dot(jnp.concatenate([hi[:, h0:h0 + lanes], lo[:, h0:h0 + lanes]], axis=1), tri)
            sums.append(out[:, :lanes] if later is None else out[:, :lanes] + later)
            later = out[:, lanes:] if later is None else later + out[:, lanes:]
        w = jnp.exp(z - jnp.concatenate(sums[::-1], axis=1))
        if first:
            w = jnp.where(diag_mask, w, 0.0)
        pv = _dot(w.astype(BF16), vp)
        later_s[p] = later
        acc_s[p] = pv if first else acc_s[p] + pv

    for p in range(n_pairs):
        chain(p, qi, True)

    def older(j, carry):
        for p in range(n_pairs):
            chain(p, qi - 1 - j, False)
        return carry

    lax.fori_loop(0, qi, older, 0)

    for p in range(n_pairs):
        acc = acc_s[p]
        oacc_ref[:, p * lanes:(p + 1) * lanes] = jnp.where(head0, acc[:blk], acc[blk:]).astype(BF16)
    o_ref[...] = x + _dot(oacc_ref[...], wo_ref[...])


def _attn(x, norm_g, layer, w_qkv, w_o):
    bsz, t_len, d = x.shape
    blk = min(ATT_BLOCK, t_len)
    n_pairs = d // V7X_LANES
    x_spec = pl.BlockSpec((None, blk, d), lambda b, i: (b, i, 0))
    return pl.pallas_call(
        functools.partial(_attn_kernel, blk=blk, d=d),
        grid=(bsz, t_len // blk),
        in_specs=[x_spec, _resident(norm_g.shape, (layer, 1)), _resident(w_qkv.shape), _resident(w_o.shape)],
        out_specs=x_spec,
        out_shape=jax.ShapeDtypeStruct(x.shape, F32),
        scratch_shapes=[pltpu.VMEM((t_len, d), BF16), pltpu.VMEM((t_len, d), BF16),
                        pltpu.VMEM((n_pairs, 2 * blk, V7X_LANES), BF16),
                        pltpu.VMEM((n_pairs, 2 * blk, V7X_LANES), F32),
                        pltpu.VMEM((n_pairs, 2 * blk, V7X_LANES), F32),
                        pltpu.VMEM((blk, d), BF16)],
        compiler_params=_cparams("arbitrary", "arbitrary"),
        name="sb_attn",
    )(x, norm_g, w_qkv, w_o)


def _conformer_kernel(x_ref, g_ref, win_ref, bin_ref, cw_ref, cb_ref, lng_ref, lnb_ref, wout_ref, bout_ref,
                      o_ref, hbuf, conv_out, *, rows, bsz, d, conv_k):
    lanes = V7X_LANES
    n_lc = d // lanes
    halo = (conv_k - 1) * bsz

    @pl.when(pl.program_id(0) == 0)
    def _():
        hbuf[:, 0:halo, :] = jnp.zeros((n_lc, halo, lanes), F32)

    x = x_ref[...]
    hn = _rms(x, g_ref[...]).astype(BF16)
    val = _dot(hn, win_ref[:, :d]) + bin_ref[:, :d]
    gt = _dot(hn, win_ref[:, d:]) + bin_ref[:, d:]
    h = val * _half_tanh_sigmoid(0.5 * gt)
    for lc in range(n_lc):
        hbuf[lc, halo:halo + rows, :] = h[:, lc * lanes:(lc + 1) * lanes]

    for lc in range(n_lc):
        cs = slice(lc * lanes, (lc + 1) * lanes)
        for s0, s1 in ((0, conv_k // 2), (conv_k // 2, conv_k)):

            def chunk(c, _, lc=lc, cs=cs, s0=s0, s1=s1):
                r0 = pl.multiple_of(c * CONV_ROWS, CONV_ROWS)
                if s0 == 0:
                    acc = jnp.broadcast_to(cb_ref[:, cs], (CONV_ROWS, lanes))
                else:
                    acc = conv_out[lc, pl.ds(r0, CONV_ROWS), :]
                for s in range(s0, s1):
                    tap = cw_ref[conv_k - 1 - s:conv_k - s, cs]
                    acc = acc + tap * hbuf[lc, pl.ds(r0 + (halo - s * bsz), CONV_ROWS), :]
                conv_out[lc, pl.ds(r0, CONV_ROWS), :] = acc
                return 0

            lax.fori_loop(0, rows // CONV_ROWS, chunk, 0)

    hbuf[:, 0:halo, :] = hbuf[:, rows:rows + halo, :]

    c = jnp.concatenate([conv_out[lc] for lc in range(n_lc)], axis=1)
    mu = jnp.mean(c, axis=-1, keepdims=True)
    cc = c - mu
    var = jnp.mean(cc * cc, axis=-1, keepdims=True)
    y = cc * lax.rsqrt(var + LN_EPS) * lng_ref[...] + lnb_ref[...]
    y = (y * _half_tanh_sigmoid(0.5 * y)).astype(BF16)
    o_ref[...] = x + _dot(y, wout_ref[...]) + bout_ref[...]


def _conformer(x_tm, bsz, norm_g, layer, j, w_in, b_in, conv_w, conv_b, ln_g, ln_b, w_out, b_out):
    n, d = x_tm.shape
    assert bsz == V7X_SUBLANES, "one time step must fill one sublane group"
    conv_k = conv_w.shape[-2]
    rows = min(SEQ_ROWS, n)
    assert (conv_k - 1) * bsz <= rows
    x_spec = pl.BlockSpec((rows, d), lambda t: (t, 0))
    args = [norm_g, w_in, b_in.reshape(-1, 1, 2 * d), conv_w, conv_b.reshape(-1, 1, d), ln_g.reshape(-1, 1, d),
            ln_b.reshape(-1, 1, d), w_out, b_out.reshape(-1, 1, d)]
    in_specs = [x_spec, _resident(norm_g.shape, (layer, 1))]
    in_specs += [_resident(a.shape, () if a.ndim == 2 else (j,)) for a in args[1:]]
    return pl.pallas_call(
        functools.partial(_conformer_kernel, rows=rows, bsz=bsz, d=d, conv_k=conv_k),
        grid=(n // rows,),
        in_specs=in_specs,
        out_specs=x_spec,
        out_shape=jax.ShapeDtypeStruct((n, d), F32),
        scratch_shapes=[pltpu.VMEM((d // V7X_LANES, rows + (conv_k - 1) * bsz, V7X_LANES), F32),
                        pltpu.VMEM((d // V7X_LANES, rows, V7X_LANES), F32)],
        compiler_params=_cparams("arbitrary"),
        name="conformer",
    )(x_tm, *args)


def kernel(x, norm_g, ffn_w_in, ffn_w_out, a_w_in, a_conv_w, a_conv_b, a_gate_w, a_gate_b, a_lambda, a_w_out,
           b_w_qkv, b_w_o, c_w_in, c_b_in, c_conv_w, c_conv_b, c_ln_g, c_ln_b, c_w_out, c_b_out, final_g):
    bsz, t_len, d = x.shape
    depth = norm_g.shape[0]
    n_mixers = 3
    attention = 1
    norm_g = norm_g.reshape(depth, 3, 1, d)
    w_in, w_out = ffn_w_in[0, 0].astype(BF16), ffn_w_out[0, 0].astype(BF16)
    a_gate_w, a_gate_b = _pack_gate_w(a_gate_w), 0.5 * a_gate_b
    mixer_w = ((a_w_in, a_w_out), (b_w_qkv, b_w_o), (c_w_in, c_w_out))

    time_major = False
    for layer in range(depth):
        kind, j = layer % n_mixers, layer // n_mixers
        mixer_tm = kind != attention
        casts = [(ffn_w_in, (layer, 1)), (ffn_w_out, (layer, 1))] + [(w, (j,)) for w in mixer_w[kind]]
        x, w_in, w_out, m_in, m_out = _ffn(x, norm_g, w_in, w_out, layer, 0, bsz, time_major, mixer_tm, casts=casts)
        if kind == 0:
            x = _rglru(x, bsz, norm_g, layer, j, m_in, a_conv_w, a_conv_b, a_gate_w, a_gate_b, a_lambda, m_out)
        elif kind == attention:
            x = _attn(x, norm_g, layer, m_in, m_out)
        else:
            x = _conformer(x, bsz, norm_g, layer, j, m_in, c_b_in, c_conv_w, c_conv_b, c_ln_g, c_ln_b, m_out,
                           c_b_out)
        last = layer == depth - 1
        time_major = (not last) and (layer + 1) % n_mixers != attention
        if last:
            (x,) = _ffn(x, norm_g, w_in, w_out, layer, 1, bsz, mixer_tm, time_major, final_g)
        else:
            casts = [(ffn_w_in, (layer + 1, 0)), (ffn_w_out, (layer + 1, 0))]
            x, w_in, w_out = _ffn(x, norm_g, w_in, w_out, layer, 1, bsz, mixer_tm, time_major, casts=casts)
    return x
```
